```python
import math
import jax, jax.numpy as jnp
from jax import lax
import numpy as np

D_MODEL = 1024
BATCH = 4
SEQ = 8192
DEPTH = 2
DEC_BATCH = 16
DEC_SEQ = 16
PAST_LEN = 2048

CHUNK = 64
Q_BLOCK = 128
NORM_EPS = 1e-6
NEG_BIG = -1e30

A_HEADS = 4
A_HEAD_DIM = 64
A_VDIM = 2 * A_HEAD_DIM
A_QK_W = A_HEADS * 2 * A_HEAD_DIM
A_V_W = A_HEADS * A_VDIM
A_SUBLN_EPS = 1e-5

B_HEADS = 8
B_HEAD_DIM = 64
B_W = B_HEADS * B_HEAD_DIM
B_DECAY_LORA = 64
B_AAA_LORA = 64
B_MV_LORA = 32
B_GATE_LORA = 160
B_GN_EPS = 64e-5
B_SPLITS = (B_W, B_W, B_W, B_DECAY_LORA, B_AAA_LORA, B_GATE_LORA)
B_COL_W = 3 * B_W + B_DECAY_LORA + B_AAA_LORA + B_GATE_LORA

C_HEADS = 4
C_EXPAND = 128
C_HEAD_DIM = 128
C_F_W = C_HEADS * C_EXPAND
C_W = C_HEADS * C_HEAD_DIM

D_FF = 2816
FFN_CONV = 3

IN_SPLITS = (A_QK_W, A_QK_W, A_V_W, B_COL_W, C_F_W, C_F_W, C_W, C_W, D_MODEL, D_MODEL, D_MODEL)
IN_W = sum(IN_SPLITS)
B_COL_START = 2 * A_QK_W + A_V_W

kernel_name = 'hybrid_diffattn_rwkv7_hgrn2_stream_step'


def _split(t, widths):
    idx = [int(i) for i in np.cumsum(widths)[:-1]]
    return jnp.split(t, idx, axis=-1)


def rmsnorm(x, w, eps=NORM_EPS):
    xf = x.astype(jnp.float32)
    y = xf * lax.rsqrt(jnp.mean(jnp.square(xf), axis=-1, keepdims=True) + eps)
    return (y * w.astype(jnp.float32)).astype(x.dtype)


def diff_lambda(lq, lk, lam_init):
    e = jnp.exp(jnp.sum(lq.astype(jnp.float32) * lk.astype(jnp.float32), axis=-1))
    return e[0] - e[1] + lam_init


def diff_attend(q, k, v, lam, mask):
    s = jnp.einsum('bqhcd,bkhcd->bhcqk', q, k).astype(jnp.float32) * (A_HEAD_DIM ** -0.5)
    if mask is not None:
        s = jnp.where(mask, s, NEG_BIG)
    p = jax.nn.softmax(s, axis=-1)
    p = p[:, :, 0] - lam * p[:, :, 1]
    return jnp.einsum('bhqk,bkhe->bqhe', p.astype(v.dtype), v)


def diff_attn_prompt(q, k, v, lam):
    n_b, n_t = q.shape[:2]
    key_chunk = jnp.arange(n_t) // CHUNK

    def one_block(i):
        start = i * Q_BLOCK
        q_blk = lax.dynamic_slice_in_dim(q, start, Q_BLOCK, axis=1)
        q_chunk = (start + jnp.arange(Q_BLOCK)) // CHUNK
        mask = key_chunk[None, :] <= q_chunk[:, None]
        return diff_attend(q_blk, k, v, lam, mask)

    o = lax.map(one_block, jnp.arange(n_t // Q_BLOCK))
    return o.transpose(1, 0, 2, 3, 4).reshape(n_b, n_t, A_HEADS, A_VDIM)


def rwkv_scan(S0, r, w, k, v, kk, b):
    def step(S, inp):
        r_t, w_t, k_t, v_t, kk_t, b_t = inp
        sa = jnp.einsum('bhvk,bhk->bhv', S, kk_t)
        S = (S * w_t[:, :, None, :] - sa[..., None] * b_t[:, :, None, :]
             + v_t[..., None] * k_t[:, :, None, :])
        return S, jnp.einsum('bhvk,bhk->bhv', S, r_t)

    xs = tuple(jnp.moveaxis(t.astype(jnp.float32), 1, 0) for t in (r, w, k, v, kk, b))
    S, ys = lax.scan(step, S0.astype(jnp.float32), xs)
    return S, jnp.moveaxis(ys, 0, 1)


def rwkv7_branch(pb, pb_prev, S0, v_first, W, l):
    n_b, n_t = pb.shape[:2]
    xs = pb + (pb_prev - pb) * W['b_mu'][l]
    r, k, v, wd, ad, gd = _split(xs, B_SPLITS)
    w_log = -jax.nn.softplus(-(W['b_w0'][l] + jnp.tanh(wd) @ W['b_w2'][l]).astype(jnp.float32)) - 0.5
    decay = jnp.exp(-jnp.exp(w_log))
    a = jax.nn.sigmoid(W['b_a0'][l] + ad @ W['b_a2'][l])
    g = jax.nn.sigmoid(gd) @ W['b_g2'][l]
    if l == 0:
        v_first = v
    else:
        v = v + (v_first - v) * jax.nn.sigmoid(W['b_v0'][l - 1] + (v @ W['b_v1'][l - 1]) @ W['b_v2'][l - 1])

    def heads(t):
        return t.astype(jnp.float32).reshape(n_b, n_t, B_HEADS, B_HEAD_DIM)

    kk = heads(k * W['b_k_k'][l])
    kk = kk / jnp.maximum(jnp.sqrt(jnp.sum(jnp.square(kk), axis=-1, keepdims=True)), 1e-12)
    k = k * (1 + (a - 1) * W['b_k_a'][l])
    rh, kh, vh, ah = heads(r), heads(k), heads(v), heads(a)
    S, y = rwkv_scan(S0, rh, heads(decay), kh, vh, kk, kk * ah)
    mean = jnp.mean(y, axis=-1, keepdims=True)
    var = jnp.mean(jnp.square(y - mean), axis=-1, keepdims=True)
    ln_w = W['b_ln_w'][l].astype(jnp.float32).reshape(B_HEADS, B_HEAD_DIM)
    ln_b = W['b_ln_b'][l].astype(jnp.float32).reshape(B_HEADS, B_HEAD_DIM)
    y = (y - mean) * lax.rsqrt(var + B_GN_EPS) * ln_w + ln_b
    y = y + jnp.sum(rh * kh * W['b_r_k'][l].astype(jnp.float32), axis=-1, keepdims=True) * vh
    o = y.reshape(n_b, n_t, B_W) * g
    return o.astype(pb.dtype), S, v_first


def gla_scan(S0, q, k, v, g):
    n_b, n_t, n_h = q.shape[:3]
    L = min(CHUNK, n_t)
    nc = n_t // L

    def to_chunks(t):
        return t.reshape(n_b, nc, L, n_h, t.shape[-1]).transpose(1, 0, 3, 2, 4)

    causal = jnp.tril(jnp.ones((L, L), dtype=bool))[:, :, None]

    def step(S, inp):
        qc, kc, vc, gc = inp
        G = jnp.cumsum(gc, axis=2)
        diff = G[:, :, :, None, :] - G[:, :, None, :, :]
        dec = jnp.where(causal, jnp.exp(jnp.where(causal, diff, 0.0)), 0.0)
        A = jnp.einsum('bhtc,bhsc,bhtsc->bhts', qc, kc, dec)
        o = jnp.einsum('bhts,bhsd->bhtd', A, vc) + jnp.einsum('bhtc,bhcd->bhtd', qc * jnp.exp(G), S)
        G_last = G[:, :, -1:, :]
        S = (jnp.exp(G_last[:, :, 0, :])[..., None] * S
             + jnp.einsum('bhsc,bhsd->bhcd', kc * jnp.exp(G_last - G), vc))
        return S, o

    S, o = lax.scan(step, S0.astype(jnp.float32), tuple(to_chunks(t) for t in (q, k, v, g)))
    return S, o.transpose(1, 0, 3, 2, 4).reshape(n_b, n_t, n_h, v.shape[-1])


def hgrn2_branch(cq, cf, ci, cog, S0, lb, W, l):
    n_b, n_t = cq.shape[:2]
    z = cf.astype(jnp.float32).reshape(n_b, n_t, C_HEADS, C_EXPAND)
    lbh = lb.reshape(C_HEADS, C_EXPAND)
    f = lbh + (1 - lbh) * jax.nn.sigmoid(z)
    log_f = jnp.log(f)
    k_in = (1 - lbh) * jax.nn.sigmoid(-z)
    q = cq.astype(jnp.float32).reshape(n_b, n_t, C_HEADS, C_EXPAND)
    v = ci.astype(jnp.float32).reshape(n_b, n_t, C_HEADS, C_HEAD_DIM)
    S, o = gla_scan(S0, q, k_in, v, log_f)
    gate = jax.nn.silu(cog.astype(jnp.float32)).reshape(n_b, n_t, C_HEADS, C_HEAD_DIM)
    o = rmsnorm(o, W['c_norm'][l]) * gate
    return o.reshape(n_b, n_t, C_W).astype(cq.dtype), S


def conv_ffn(xn, prev, W, l):
    n_t = xn.shape[1]
    up, gate = jnp.split(xn @ W['ffn_up'][l], 2, axis=-1)
    up_pad = jnp.concatenate([prev.astype(up.dtype), up], axis=1)
    cw = W['ffn_conv_w'][l]
    c = W['ffn_conv_b'][l] + cw[0] * up_pad[:, 0:n_t]
    for j in range(1, FFN_CONV):
        c = c + cw[j] * up_pad[:, j:j + n_t]
    h = jax.nn.gelu(c, approximate=False) * gate
    return h @ W['ffn_down'][l], up_pad[:, n_t:]


def block(x, l, W, lb, v_first, shift_row, kv_cache, S_rwkv, S_hgrn, conv_prev):
    n_b, n_t = x.shape[:2]
    w_in = W['in_proj'][l]
    xn = rmsnorm(x, W['mix_norm'][l])
    aq, ak, av, pb, cq, cf, ci, cog, ga, gb, gc = _split(xn @ w_in, IN_SPLITS)

    q = aq.reshape(n_b, n_t, A_HEADS, 2, A_HEAD_DIM)
    k = ak.reshape(n_b, n_t, A_HEADS, 2, A_HEAD_DIM)
    v = av.reshape(n_b, n_t, A_HEADS, A_VDIM)
    lam_init = 0.8 - 0.6 * math.exp(-0.3 * l)
    lam = diff_lambda(W['a_lq'][l], W['a_lk'][l], lam_init)
    if kv_cache is None:
        o = diff_attn_prompt(q, k, v, lam)
    else:
        ck, cv = kv_cache
        k_all = jnp.concatenate([ck.astype(k.dtype), k], axis=1)
        v_all = jnp.concatenate([cv.astype(v.dtype), v], axis=1)
        o = diff_attend(q, k_all, v_all, lam, None)
    oa = (rmsnorm(o, W['a_subln'][l], A_SUBLN_EPS) * (1 - lam_init)).reshape(n_b, n_t, A_V_W)

    if shift_row is None:
        prev0 = jnp.zeros((n_b, 1, B_COL_W), pb.dtype)
    else:
        prev0 = (shift_row @ w_in[:, B_COL_START:B_COL_START + B_COL_W])[:, None]
    pb_prev = jnp.concatenate([prev0.astype(pb.dtype), pb[:, :-1]], axis=1)
    ob, S_rwkv, v_first = rwkv7_branch(pb, pb_prev, S_rwkv, v_first, W, l)

    oc, S_hgrn = hgrn2_branch(cq, cf, ci, cog, S_hgrn, lb, W, l)

    m = (jax.nn.sigmoid(ga) * (oa @ W['proj_a'][l])
         + jax.nn.sigmoid(gb) * (ob @ W['proj_b'][l])
         + jax.nn.sigmoid(gc) * (oc @ W['proj_c'][l]))
    x = x + (m @ W['out_proj'][l]).astype(x.dtype)

    f, conv_new = conv_ffn(rmsnorm(x, W['ffn_norm'][l]), conv_prev, W, l)
    x = x + f.astype(x.dtype)
    return x, v_first, (k, v, xn[:, -1], S_rwkv, S_hgrn, conv_new)


def _stack_layers(states):
    return [jnp.stack([s[i] for s in states]) for i in range(len(states[0]))]


def setup_inputs(seed: int = 0) -> dict:
    key = jax.random.key(seed)
    ks = iter(jax.random.split(key, 64))

    def nrm(shape, scale):
        return scale * jax.random.normal(next(ks), shape, jnp.float32)

    def uni(shape, lo, hi):
        return jax.random.uniform(next(ks), shape, jnp.float32, lo, hi)

    return {
        'x_prompt': nrm((BATCH, SEQ, D_MODEL), 1.0),
        'x_sample': nrm((DEC_BATCH, DEC_SEQ, D_MODEL), 1.0),
        'cache_attn_k': nrm((DEPTH, DEC_BATCH, PAST_LEN, A_HEADS, 2, A_HEAD_DIM), 1.0),
        'cache_attn_v': nrm((DEPTH, DEC_BATCH, PAST_LEN, A_HEADS, A_VDIM), 1.0),
        'state_rwkv_shift': nrm((DEPTH, DEC_BATCH, D_MODEL), 1.0),
        'state_rwkv_wkv': nrm((DEPTH, DEC_BATCH, B_HEADS, B_HEAD_DIM, B_HEAD_DIM), 0.3),
        'state_hgrn': nrm((DEPTH, DEC_BATCH, C_HEADS, C_EXPAND, C_HEAD_DIM), 0.3),
        'state_ffn_conv': nrm((DEPTH, DEC_BATCH, FFN_CONV - 1, D_FF), 1.0),
        'mix_norm': 1.0 + nrm((DEPTH, D_MODEL), 0.02),
        'in_proj': nrm((DEPTH, D_MODEL, IN_W), D_MODEL ** -0.5),
        'a_lq': nrm((DEPTH, 2, A_HEAD_DIM), 0.1),
        'a_lk': nrm((DEPTH, 2, A_HEAD_DIM), 0.1),
        'a_subln': 1.0 + nrm((DEPTH, A_VDIM), 0.02),
        'b_mu': uni((DEPTH, B_COL_W), 0.0, 1.0),
        'b_w0': uni((DEPTH, B_W), -6.0, -1.0),
        'b_w2': nrm((DEPTH, B_DECAY_LORA, B_W), 0.1),
        'b_a0': nrm((DEPTH, B_W), 0.1),
        'b_a2': nrm((DEPTH, B_AAA_LORA, B_W), 0.5 * B_AAA_LORA ** -0.5),
        'b_g2': nrm((DEPTH, B_GATE_LORA, B_W), B_GATE_LORA ** -0.5),
        'b_k_k': 0.85 + nrm((DEPTH, B_W), 0.02),
        'b_k_a': 1.0 + nrm((DEPTH, B_W), 0.02),
        'b_r_k': nrm((DEPTH, B_HEADS, B_HEAD_DIM), 0.1),
        'b_ln_w': 1.0 + nrm((DEPTH, B_W), 0.02),
        'b_ln_b': nrm((DEPTH, B_W), 0.02),
        'b_v0': nrm((DEPTH - 1, B_W), 0.1),
        'b_v1': nrm((DEPTH - 1, B_W, B_MV_LORA), B_W ** -0.5),
        'b_v2': nrm((DEPTH - 1, B_MV_LORA, B_W), B_MV_LORA ** -0.5),
        'c_lb': nrm((DEPTH, C_F_W), 0.1),
        'c_norm': 1.0 + nrm((DEPTH, C_HEAD_DIM), 0.02),
        'proj_a': nrm((DEPTH, A_V_W, D_MODEL), A_V_W ** -0.5),
        'proj_b': nrm((DEPTH, B_W, D_MODEL), B_W ** -0.5),
        'proj_c': nrm((DEPTH, C_W, D_MODEL), C_W ** -0.5),
        'out_proj': nrm((DEPTH, D_MODEL, D_MODEL), D_MODEL ** -0.5),
        'ffn_norm': 1.0 + nrm((DEPTH, D_MODEL), 0.02),
        'ffn_up': nrm((DEPTH, D_MODEL, 2 * D_FF), D_MODEL ** -0.5),
        'ffn_conv_w': nrm((DEPTH, FFN_CONV, D_FF), 0.5),
        'ffn_conv_b': nrm((DEPTH, D_FF), 0.02),
        'ffn_down': nrm((DEPTH, D_FF, D_MODEL), D_FF ** -0.5),
        'final_norm': 1.0 + nrm((D_MODEL,), 0.02),
    }


def reference(x_prompt, x_sample, cache_attn_k, cache_attn_v, state_rwkv_shift, state_rwkv_wkv,
              state_hgrn, state_ffn_conv, mix_norm, in_proj, a_lq, a_lk, a_subln, b_mu, b_w0, b_w2,
              b_a0, b_a2, b_g2, b_k_k, b_k_a, b_r_k, b_ln_w, b_ln_b, b_v0, b_v1, b_v2, c_lb, c_norm,
              proj_a, proj_b, proj_c, out_proj, ffn_norm, ffn_up, ffn_conv_w, ffn_conv_b, ffn_down,
              final_norm):
    W = dict(mix_norm=mix_norm, in_proj=in_proj, a_lq=a_lq, a_lk=a_lk, a_subln=a_subln,
             b_mu=b_mu, b_w0=b_w0, b_w2=b_w2, b_a0=b_a0, b_a2=b_a2, b_g2=b_g2, b_k_k=b_k_k,
             b_k_a=b_k_a, b_r_k=b_r_k, b_ln_w=b_ln_w, b_ln_b=b_ln_b, b_v0=b_v0, b_v1=b_v1,
             b_v2=b_v2, c_norm=c_norm, proj_a=proj_a, proj_b=proj_b, proj_c=proj_c,
             out_proj=out_proj, ffn_norm=ffn_norm, ffn_up=ffn_up, ffn_conv_w=ffn_conv_w,
             ffn_conv_b=ffn_conv_b, ffn_down=ffn_down)
    lb_soft = jax.nn.softmax(c_lb.astype(jnp.float32), axis=0)
    lb_all = jnp.cumsum(lb_soft, axis=0) - lb_soft[0]

    n_p = x_prompt.shape[0]
    xp, xs = x_prompt, x_sample
    vf_p = None
    vf_s = None
    st_p, st_s = [], []
    for l in range(DEPTH):
        xp, vf_p, s_p = block(xp, l, W, lb_all[l], vf_p, None, None,
                              jnp.zeros((n_p, B_HEADS, B_HEAD_DIM, B_HEAD_DIM), jnp.float32),
                              jnp.zeros((n_p, C_HEADS, C_EXPAND, C_HEAD_DIM), jnp.float32),
                              jnp.zeros((n_p, FFN_CONV - 1, D_FF), x_prompt.dtype))
        st_p.append(s_p)
        xs, vf_s, s_s = block(xs, l, W, lb_all[l], vf_s, state_rwkv_shift[l],
                              (cache_attn_k[l], cache_attn_v[l]), state_rwkv_wkv[l],
                              state_hgrn[l], state_ffn_conv[l])
        st_s.append(s_s)

    y_prompt = rmsnorm(xp, final_norm)
    y_sample = rmsnorm(xs, final_norm)
    attn_k_p, attn_v_p, rwkv_shift_p, rwkv_wkv_p, hgrn_p, ffn_conv_p = _stack_layers(st_p)
    attn_k_s, attn_v_s, rwkv_shift_s, rwkv_wkv_s, hgrn_s, ffn_conv_s = _stack_layers(st_s)
    return (y_prompt, y_sample, attn_k_p, attn_v_p, rwkv_shift_p, rwkv_wkv_p, hgrn_p, ffn_conv_p,
            attn_k_s, attn_v_s, rwkv_shift_s, rwkv_wkv_s, hgrn_s, ffn_conv_s)
```

```python
import functools
import math

import numpy as np
import jax
import jax.numpy as jnp
from jax import lax
from jax.experimental import pallas as pl
from jax.experimental.pallas import tpu as pltpu

F32 = jnp.float32
MXU_DTYPE = jnp.bfloat16

NORM_EPS = 1e-6
NEG_BIG = -1e30
CHUNK = 64

A_HEADS = 4
A_HEAD_DIM = 64
A_VDIM = 128
A_SUBLN_EPS = 1e-5

B_HEADS = 8
B_HEAD_DIM = 64
B_W = 512
B_DECAY_LORA = 64
B_AAA_LORA = 64
B_MV_LORA = 32
B_GATE_LORA = 160
B_GN_EPS = 64e-5

C_HEADS = 4
C_DIM = 128
C_W = 512

D_FF = 2816
FFN_CONV = 3

LANE = 128
VMEM_LIMIT = 56 * 1024 * 1024

COL_GA, COL_GB, COL_GC = 0, 1024, 2048
COL_RKV = 3072
COL_AQ, COL_AK, COL_AV = 4608, 5120, 5632
COL_CQ, COL_CF, COL_CI, COL_COG = 6144, 6656, 7168, 7680
COL_GD = 8192
COL_WA = 8448
WP = 8704
GD_W = 256
WA_W = 128
RKV_W = 3 * B_W


def _cparams(sem):
    return pltpu.CompilerParams(dimension_semantics=sem, vmem_limit_bytes=VMEM_LIMIT)


def _sigmoid(x):
    return 1.0 / (1.0 + jnp.exp(-x))


def _rms(x, w, eps):
    return x * lax.rsqrt(jnp.mean(x * x, axis=-1, keepdims=True) + eps) * w


_NN = (((1,), (0,)), ((), ()))
_NT = (((1,), (1,)), ((), ()))
_TN = (((0,), (0,)), ((), ()))


def _dot(a, b, dims=_NN):
    return lax.dot_general(a.astype(MXU_DTYPE), b.astype(MXU_DTYPE), dims,
                           preferred_element_type=F32)


def _split2(a):
    hi = a.astype(MXU_DTYPE)
    lo = (a - hi.astype(F32)).astype(MXU_DTYPE)
    return hi, lo


def _dot3(a, b, dims=_NN):
    ah, al = _split2(a)
    bh, bl = _split2(b)
    d = lambda x, y: lax.dot_general(x, y, dims, preferred_element_type=F32)
    return d(ah, bh) + (d(ah, bl) + d(al, bh))


def _norm_matmul_kernel(x_ref, g_ref, w_ref, o_ref, xn_ref, *, apply_norm):
    @pl.when(pl.program_id(1) == 0)
    def _():
        x = x_ref[...]
        if apply_norm:
            x = _rms(x, g_ref[...], NORM_EPS)
        xn_ref[...] = x.astype(xn_ref.dtype)

    o_ref[...] = jnp.dot(xn_ref[...], w_ref[...], preferred_element_type=F32)


def _norm_matmul(x, g, w, *, apply_norm, tm, tn):
    n, d = x.shape
    wn = w.shape[1]
    return pl.pallas_call(
        functools.partial(_norm_matmul_kernel, apply_norm=apply_norm),
        name="norm_matmul",
        grid=(n // tm, wn // tn),
        in_specs=[pl.BlockSpec((tm, d), lambda i, j: (i, 0)),
                  pl.BlockSpec((1, d), lambda i, j: (0, 0)),
                  pl.BlockSpec((d, tn), lambda i, j: (0, j))],
        out_specs=pl.BlockSpec((tm, tn), lambda i, j: (i, j)),
        out_shape=jax.ShapeDtypeStruct((n, wn), F32),
        scratch_shapes=[pltpu.VMEM((tm, d), w.dtype)],
        compiler_params=_cparams(("parallel", "arbitrary")),
    )(x, g, w)


def _rmsnorm_kernel(x_ref, g_ref, o_ref):
    o_ref[...] = _rms(x_ref[...], g_ref[...], NORM_EPS)


def _rmsnorm(x, g, *, tm):
    n, d = x.shape
    return pl.pallas_call(
        _rmsnorm_kernel,
        name="rmsnorm",
        grid=(n // tm,),
        in_specs=[pl.BlockSpec((tm, d), lambda i: (i, 0)),
                  pl.BlockSpec((1, d), lambda i: (0, 0))],
        out_specs=pl.BlockSpec((tm, d), lambda i: (i, 0)),
        out_shape=jax.ShapeDtypeStruct((n, d), F32),
        compiler_params=_cparams(("parallel",)),
    )(x, g)


def _map_queries(q):
    q = q * (A_HEAD_DIM ** -0.5)
    first = lax.broadcasted_iota(jnp.int32, q.shape, 1) < A_HEAD_DIM
    return (jnp.where(first, q, 0.0).astype(MXU_DTYPE),
            jnp.where(first, 0.0, q).astype(MXU_DTYPE))


def _diff_lambda(lq_ref, lk_ref, lam_init):
    e = jnp.exp(jnp.sum(lq_ref[...] * lk_ref[...], axis=-1, keepdims=True))
    return e[0:1] - e[1:2] + lam_init


def _attn_finish(o1, o2, lam, subln, lam_init):
    o = o1 - lam * o2
    return _rms(o, subln, A_SUBLN_EPS) * (1.0 - lam_init)


def _attn_prompt_kernel(q_ref, k_ref, v_ref, lq_ref, lk_ref, sub_ref, o_ref, *, tq, lam_init):
    qi = pl.program_id(2)
    qs = _map_queries(q_ref[0])
    log2_chunk = int(round(math.log2(CHUNK)))

    def block(kstart, carry, masked):
        k = k_ref[0, pl.ds(kstart, tq), :].astype(MXU_DTYPE)
        v = v_ref[0, pl.ds(kstart, tq), :].astype(MXU_DTYPE)
        if masked:
            row_chunk = lax.broadcasted_iota(jnp.int32, (tq, tq), 0) >> log2_chunk
            col_chunk = lax.broadcasted_iota(jnp.int32, (tq, tq), 1) >> log2_chunk
            visible = col_chunk <= row_chunk
        new = []
        for c in range(2):
            m_prev, l_prev, acc_prev = carry[c]
            s = lax.dot_general(qs[c], k, _NT, preferred_element_type=F32)
            if masked:
                s = jnp.where(visible, s, NEG_BIG)
            m_new = jnp.maximum(m_prev, jnp.max(s, axis=-1, keepdims=True))
            alpha = jnp.exp(m_prev - m_new)
            p = jnp.exp(s - m_new)
            l_new = alpha * l_prev + jnp.sum(p, axis=-1, keepdims=True)
            acc_new = alpha * acc_prev + jnp.dot(p.astype(MXU_DTYPE), v,
                                                 preferred_element_type=F32)
            new.append((m_new, l_new, acc_new))
        return tuple(new)

    init = tuple((jnp.full((tq, 1), NEG_BIG, F32), jnp.zeros((tq, 1), F32),
                  jnp.zeros((tq, A_VDIM), F32)) for _ in range(2))
    carry = lax.fori_loop(
        0, qi, lambda i, c: block(pl.multiple_of(i * tq, tq), c, False), init)
    (_, l1, acc1), (_, l2, acc2) = block(pl.multiple_of(qi * tq, tq), carry, True)
    lam = _diff_lambda(lq_ref, lk_ref, lam_init)
    o_ref[0] = _attn_finish(acc1 / l1, acc2 / l2, lam, sub_ref[...], lam_init)


def _attn_prompt(p3, lq, lk, subln, *, lam_init, tq):
    nb, nt, _ = p3.shape
    cq, ck, cv = COL_AQ // LANE, COL_AK // LANE, COL_AV // LANE
    const = lambda shape: pl.BlockSpec(shape, lambda b, h, qi: (0, 0))
    return pl.pallas_call(
        functools.partial(_attn_prompt_kernel, tq=tq, lam_init=lam_init),
        name="attn_prompt",
        grid=(nb, A_HEADS, nt // tq),
        in_specs=[
            pl.BlockSpec((1, tq, LANE), lambda b, h, qi: (b, qi, cq + h)),
            pl.BlockSpec((1, nt, LANE), lambda b, h, qi: (b, 0, ck + h)),
            pl.BlockSpec((1, nt, LANE), lambda b, h, qi: (b, 0, cv + h)),
            const((2, A_HEAD_DIM)), const((2, A_HEAD_DIM)), const((1, A_VDIM)),
        ],
        out_specs=pl.BlockSpec((1, tq, LANE), lambda b, h, qi: (b, qi, h)),
        out_shape=jax.ShapeDtypeStruct((nb, nt, A_HEADS * A_VDIM), F32),
        compiler_params=_cparams(("parallel", "parallel", "arbitrary")),
    )(p3, p3, p3, lq, lk, subln)


def _attn_sample_kernel(q_ref, kn_ref, vn_ref, kc_ref, vc_ref, lq_ref, lk_ref, sub_ref, o_ref,
                        *, lam_init):
    qs = _map_queries(q_ref[0])
    kn = kn_ref[0].astype(MXU_DTYPE)
    vn = vn_ref[0].astype(MXU_DTYPE)
    kc = kc_ref[0].astype(MXU_DTYPE)
    vc = vc_ref[0].astype(MXU_DTYPE)
    outs = []
    for c in range(2):
        s_c = lax.dot_general(qs[c], kc, _NT, preferred_element_type=F32)
        s_n = lax.dot_general(qs[c], kn, _NT, preferred_element_type=F32)
        m = jnp.maximum(jnp.max(s_c, axis=-1, keepdims=True), jnp.max(s_n, axis=-1, keepdims=True))
        p_c = jnp.exp(s_c - m)
        p_n = jnp.exp(s_n - m)
        l = jnp.sum(p_c, axis=-1, keepdims=True) + jnp.sum(p_n, axis=-1, keepdims=True)
        acc = (jnp.dot(p_c.astype(MXU_DTYPE), vc, preferred_element_type=F32)
               + jnp.dot(p_n.astype(MXU_DTYPE), vn, preferred_element_type=F32))
        outs.append(acc / l)
    lam = _diff_lambda(lq_ref, lk_ref, lam_init)
    o_ref[0] = _attn_finish(outs[0], outs[1], lam, sub_ref[...], lam_init)


def _attn_sample(p3, cache_k, cache_v, lq, lk, subln, *, lam_init):
    nb, nt, _ = p3.shape
    past = cache_k.shape[1]
    cq, ck, cv = COL_AQ // LANE, COL_AK // LANE, COL_AV // LANE
    return pl.pallas_call(
        functools.partial(_attn_sample_kernel, lam_init=lam_init),
        name="attn_sample",
        grid=(nb, A_HEADS),
        in_specs=[
            pl.BlockSpec((1, nt, LANE), lambda b, h: (b, 0, cq + h)),
            pl.BlockSpec((1, nt, LANE), lambda b, h: (b, 0, ck + h)),
            pl.BlockSpec((1, nt, LANE), lambda b, h: (b, 0, cv + h)),
            pl.BlockSpec((1, past, LANE), lambda b, h: (b, 0, h)),
            pl.BlockSpec((1, past, LANE), lambda b, h: (b, 0, h)),
            pl.BlockSpec((2, A_HEAD_DIM), lambda b, h: (0, 0)),
            pl.BlockSpec((2, A_HEAD_DIM), lambda b, h: (0, 0)),
            pl.BlockSpec((1, A_VDIM), lambda b, h: (0, 0)),
        ],
        out_specs=pl.BlockSpec((1, nt, LANE), lambda b, h: (b, 0, h)),
        out_shape=jax.ShapeDtypeStruct((nb, nt, A_HEADS * A_VDIM), F32),
        compiler_params=_cparams(("parallel", "parallel")),
    )(p3, p3, p3, cache_k, cache_v, lq, lk, subln)


def _cumsum_rows(x):
    n = x.shape[0]
    row = lax.broadcasted_iota(jnp.int32, x.shape, 0)
    s = 1
    while s < n:
        x = x + jnp.where(row >= s, pltpu.roll(x, s, 0), 0.0)
        s *= 2
    return x


def _shift_lerp(x, carry, mu):
    row = lax.broadcasted_iota(jnp.int32, x.shape, 0)
    prev = jnp.where(row == 0, carry, pltpu.roll(x, 1, 0))
    return x + (prev - x) * mu


def _rwkv_kernel(*refs, L, vmix):
    it = iter(refs)
    rkv_ref, gd_ref, wa_ref = next(it), next(it), next(it)
    p_rkv_ref, p_gd_ref, p_wa_ref = next(it), next(it), next(it)
    s0_ref = next(it)
    vf_ref = next(it) if vmix else None
    mu_rkv, mu_gd, mu_wa = next(it), next(it), next(it)
    w0, w2p, a0, a2p, g2p = next(it), next(it), next(it), next(it), next(it)
    kk_w, ka_w, rk_w, lnw, lnb = next(it), next(it), next(it), next(it), next(it)
    if vmix:
        v0, v1p, v2p = next(it), next(it), next(it)
    o_ref = next(it)
    vout_ref = None if vmix else next(it)
    sout_ref = next(it)
    s_sc, c_rkv, c_gd, c_wa = next(it), next(it), next(it), next(it)

    t = pl.program_id(1)

    @pl.when(t == 0)
    def _():
        s_sc[...] = s0_ref[0]
        c_rkv[...] = p_rkv_ref[0]
        c_gd[...] = p_gd_ref[0]
        c_wa[...] = p_wa_ref[0]

    x_rkv, x_gd, x_wa = rkv_ref[0], gd_ref[0], wa_ref[0]
    xs_rkv = _shift_lerp(x_rkv, c_rkv[...], mu_rkv[...])
    xs_gd = _shift_lerp(x_gd, c_gd[...], mu_gd[...])
    xs_wa = _shift_lerp(x_wa, c_wa[...], mu_wa[...])
    c_rkv[...] = x_rkv[L - 1:L]
    c_gd[...] = x_gd[L - 1:L]
    c_wa[...] = x_wa[L - 1:L]

    r = xs_rkv[:, 0:B_W]
    k = xs_rkv[:, B_W:2 * B_W]
    v = xs_rkv[:, 2 * B_W:3 * B_W]

    zw = -(w0[...] + _dot(jnp.tanh(xs_wa), w2p[...]))
    softplus = jnp.maximum(zw, 0.0) + jnp.log(1.0 + jnp.exp(-jnp.abs(zw)))
    lw = -jnp.exp(-softplus - 0.5)
    a = _sigmoid(a0[...] + _dot(xs_wa, a2p[...]))
    g = _dot(_sigmoid(xs_gd), g2p[...])
    if vmix:
        mix = _sigmoid(v0[...] + _dot(_dot(v, v1p[...]), v2p[...]))
        v = v + (vf_ref[0] - v) * mix
    else:
        vout_ref[0] = v

    kk_raw = k * kk_w[...]
    kp = k * (1.0 + (a - 1.0) * ka_w[...])

    G = _cumsum_rows(lw)
    GL = G[L - 1:L]
    e_g = jnp.exp(G)
    e_gx = jnp.exp(G - lw)
    e_ng = jnp.exp(-G)
    e_glg = jnp.exp(GL - G)
    e_gl = jnp.exp(GL)

    row = lax.broadcasted_iota(jnp.int32, (L, L), 0)
    col = lax.broadcasted_iota(jnp.int32, (L, L), 1)
    strict = row > col
    incl = row >= col
    eye = jnp.where(row == col, 1.0, 0.0)
    n_sq = int(round(math.log2(L))) - 1

    for h in range(B_HEADS):
        sl = slice(h * B_HEAD_DIM, (h + 1) * B_HEAD_DIM)
        r_h, v_h, kp_h, a_h = r[:, sl], v[:, sl], kp[:, sl], a[:, sl]
        kk_h = kk_raw[:, sl]
        kk_h = kk_h / jnp.maximum(jnp.sqrt(jnp.sum(kk_h * kk_h, axis=-1, keepdims=True)), 1e-12)
        beta = -(kk_h * a_h)

        lhs = jnp.concatenate([kk_h * e_gx[:, sl], r_h * e_g[:, sl]], axis=0)
        m_b = _dot3(lhs, beta * e_ng[:, sl], _NT)
        m_k = _dot3(lhs, kp_h * e_ng[:, sl], _NT)
        n_ab = jnp.where(strict, m_b[0:L], 0.0)
        a_ak = jnp.where(strict, m_k[0:L], 0.0)
        a_rb = jnp.where(incl, m_b[L:2 * L], 0.0)
        a_rk = jnp.where(incl, m_k[L:2 * L], 0.0)

        tinv = eye + n_ab
        n_pow = n_ab
        for _ in range(n_sq):
            n_pow = _dot3(n_pow, n_pow)
            tinv = tinv + _dot3(tinv, n_pow)

        s0 = s_sc[h]
        ls = _dot3(lhs, s0, _NT)
        u = _dot3(tinv, ls[0:L] + _dot3(a_ak, v_h))
        y = ls[L:2 * L] + _dot3(a_rb, u) + _dot3(a_rk, v_h)
        s_new = (s0 * e_gl[:, sl] + _dot3(u, beta * e_glg[:, sl], _TN)
                 + _dot3(v_h, kp_h * e_glg[:, sl], _TN))
        s_sc[h] = s_new

        mean = jnp.mean(y, axis=-1, keepdims=True)
        var = jnp.mean(jnp.square(y - mean), axis=-1, keepdims=True)
        y = (y - mean) * lax.rsqrt(var + B_GN_EPS) * lnw[:, sl] + lnb[:, sl]
        y = y + jnp.sum(r_h * kp_h * rk_w[:, sl], axis=-1, keepdims=True) * v_h
        o_ref[0, :, sl] = y * g[:, sl]

    @pl.when(t == pl.num_programs(1) - 1)
    def _():
        sout_ref[0] = s_sc[...]


def _rwkv(p3, prev3, s0, v_first, wts, *, L):
    nb, nt, _ = p3.shape
    vmix = v_first is not None
    c_rkv, c_gd, c_wa = COL_RKV // RKV_W, COL_GD // GD_W, COL_WA // WA_W
    const = lambda shape: pl.BlockSpec(shape, lambda b, t: (0,) * len(shape))
    in_specs = [
        pl.BlockSpec((1, L, RKV_W), lambda b, t: (b, t, c_rkv)),
        pl.BlockSpec((1, L, GD_W), lambda b, t: (b, t, c_gd)),
        pl.BlockSpec((1, L, WA_W), lambda b, t: (b, t, c_wa)),
        pl.BlockSpec((1, 1, RKV_W), lambda b, t: (b, 0, c_rkv)),
        pl.BlockSpec((1, 1, GD_W), lambda b, t: (b, 0, c_gd)),
        pl.BlockSpec((1, 1, WA_W), lambda b, t: (b, 0, c_wa)),
        pl.BlockSpec((1, B_HEADS, B_HEAD_DIM, B_HEAD_DIM), lambda b, t: (b, 0, 0, 0)),
    ]
    args = [p3, p3, p3, prev3, prev3, prev3, s0]
    if vmix:
        in_specs.append(pl.BlockSpec((1, L, B_W), lambda b, t: (b, t, 0)))
        args.append(v_first)
    names = ["mu_rkv", "mu_gd", "mu_wa", "w0", "w2p", "a0", "a2p", "g2p",
             "kk_w", "ka_w", "rk_w", "lnw", "lnb"]
    if vmix:
        names += ["v0", "v1p", "v2p"]
    for nme in names:
        in_specs.append(const(wts[nme].shape))
        args.append(wts[nme])
    seq_spec = pl.BlockSpec((1, L, B_W), lambda b, t: (b, t, 0))
    seq_shape = jax.ShapeDtypeStruct((nb, nt, B_W), F32)
    st_spec = pl.BlockSpec((1, B_HEADS, B_HEAD_DIM, B_HEAD_DIM), lambda b, t: (b, 0, 0, 0))
    st_shape = jax.ShapeDtypeStruct((nb, B_HEADS, B_HEAD_DIM, B_HEAD_DIM), F32)
    if vmix:
        out_specs, out_shape = [seq_spec, st_spec], [seq_shape, st_shape]
    else:
        out_specs, out_shape = [seq_spec, seq_spec, st_spec], [seq_shape, seq_shape, st_shape]
    outs = pl.pallas_call(
        functools.partial(_rwkv_kernel, L=L, vmix=vmix),
        name="rwkv7",
        grid=(nb, nt // L),
        in_specs=in_specs, out_specs=out_specs, out_shape=out_shape,
        scratch_shapes=[pltpu.VMEM((B_HEADS, B_HEAD_DIM, B_HEAD_DIM), F32),
                        pltpu.VMEM((1, RKV_W), F32), pltpu.VMEM((1, GD_W), F32),
                        pltpu.VMEM((1, WA_W), F32)],
        compiler_params=_cparams(("parallel", "arbitrary")),
    )(*args)
    if vmix:
        return outs[0], v_first, outs[1]
    return outs[0], outs[1], outs[2]


def _hgrn_levels(L):
    return [L >> (j + 1) for j in range(int(round(math.log2(L))))]


def _hgrn_sum_masks(L):
    t = np.arange(L)[:, None]
    i = np.arange(L)[None, :]
    blocks = []
    for b in _hgrn_levels(L):
        mid = (t // (2 * b)) * (2 * b) + b - 1
        upper = (t % (2 * b)) >= b
        blocks.append(upper & (i > mid) & (i <= t))
        blocks.append((~upper) & (i > t) & (i <= mid))
    blocks.append(i <= t)
    blocks.append(i > t)
    return np.concatenate(blocks, axis=0).astype(np.float32)


def _hgrn_kernel(cq_ref, cf_ref, ci_ref, cog_ref, s0_ref, clb_ref, cnorm_ref, mm_ref,
                 o_ref, sout_ref, st_sc, *, L, layer):
    t = pl.program_id(1)

    @pl.when(t == 0)
    def _():
        for h in range(C_HEADS):
            st_sc[h] = s0_ref[0, h].T

    clb = clb_ref[...]
    e = jnp.exp(clb - jnp.max(clb, axis=0, keepdims=True))
    soft = e / jnp.sum(e, axis=0, keepdims=True)
    lb = jnp.zeros((1, C_W), F32)
    for i in range(1, layer + 1):
        lb = lb + soft[i:i + 1]

    z, q, v = cf_ref[0], cq_ref[0], ci_ref[0]
    f = lb + (1.0 - lb) * _sigmoid(z)
    log_f = jnp.log(f)
    k = (1.0 - lb) * _sigmoid(-z)

    hi = log_f.astype(MXU_DTYPE)
    r1 = log_f - hi.astype(F32)
    mid = r1.astype(MXU_DTYPE)
    lo = (r1 - mid.astype(F32)).astype(MXU_DTYPE)
    mm = mm_ref[...]
    d = lambda x: jnp.dot(mm, x, preferred_element_type=F32)
    sums = d(hi) + (d(mid) + d(lo))

    levels = _hgrn_levels(L)
    nl = len(levels)
    g_incl = sums[2 * nl * L:(2 * nl + 1) * L]
    g_suf = sums[(2 * nl + 1) * L:(2 * nl + 2) * L]
    g_last = g_incl[L - 1:L]
    q_in = q * jnp.exp(g_incl)
    k_out = k * jnp.exp(g_suf)
    e_last = jnp.exp(g_last)

    rowv = lax.broadcasted_iota(jnp.int32, (L, 1), 0)
    row = lax.broadcasted_iota(jnp.int32, (L, L), 0)
    col = lax.broadcasted_iota(jnp.int32, (L, L), 1)
    qt, kt, lmask = [], [], []
    for j, b in enumerate(levels):
        upper = (rowv & b) != 0
        qt.append(jnp.where(upper, q * jnp.exp(sums[2 * j * L:(2 * j + 1) * L]), 0.0))
        kt.append(jnp.where(upper, 0.0, k * jnp.exp(sums[(2 * j + 1) * L:(2 * j + 2) * L])))
        sh = int(round(math.log2(2 * b)))
        lmask.append(((row >> sh) == (col >> sh)) & ((row & b) != 0) & ((col & b) == 0))
    diag = row == col

    gate_in = cog_ref[0]
    for h in range(C_HEADS):
        sl = slice(h * C_DIM, (h + 1) * C_DIM)
        a_mat = jnp.where(diag, jnp.sum(q[:, sl] * k[:, sl], axis=-1, keepdims=True), 0.0)
        for j in range(nl):
            a_mat = a_mat + jnp.where(lmask[j], _dot(qt[j][:, sl], kt[j][:, sl], _NT), 0.0)
        st = st_sc[h]
        o = _dot(a_mat, v[:, sl]) + _dot(q_in[:, sl], st, _NT)
        st_sc[h] = st * e_last[:, sl] + _dot(v[:, sl], k_out[:, sl], _TN)
        gate = gate_in[:, sl]
        o_ref[0, :, sl] = _rms(o, cnorm_ref[...], NORM_EPS) * (gate * _sigmoid(gate))

    @pl.when(t == pl.num_programs(1) - 1)
    def _():
        for h in range(C_HEADS):
            sout_ref[0, h] = st_sc[h].T


def _hgrn(p3, s0, c_lb, c_norm, *, L, layer):
    nb, nt, _ = p3.shape
    mm = jnp.asarray(_hgrn_sum_masks(L), MXU_DTYPE)
    cols = [COL_CQ // C_W, COL_CF // C_W, COL_CI // C_W, COL_COG // C_W]
    seq = lambda c: pl.BlockSpec((1, L, C_W), lambda b, t: (b, t, c))
    st_spec = pl.BlockSpec((1, C_HEADS, C_DIM, C_DIM), lambda b, t: (b, 0, 0, 0))
    return pl.pallas_call(
        functools.partial(_hgrn_kernel, L=L, layer=layer),
        name="hgrn2",
        grid=(nb, nt // L),
        in_specs=[seq(cols[0]), seq(cols[1]), seq(cols[2]), seq(cols[3]), st_spec,
                  pl.BlockSpec(c_lb.shape, lambda b, t: (0, 0)),
                  pl.BlockSpec((1, C_DIM), lambda b, t: (0, 0)),
                  pl.BlockSpec(mm.shape, lambda b, t: (0, 0))],
        out_specs=[pl.BlockSpec((1, L, C_W), lambda b, t: (b, t, 0)), st_spec],
        out_shape=[jax.ShapeDtypeStruct((nb, nt, C_W), F32),
                   jax.ShapeDtypeStruct((nb, C_HEADS, C_DIM, C_DIM), F32)],
        scratch_shapes=[pltpu.VMEM((C_HEADS, C_DIM, C_DIM), F32)],
        compiler_params=_cparams(("parallel", "arbitrary")),
    )(p3, p3, p3, p3, s0, c_lb, c_norm, mm)


def _merge_kernel(oa_ref, ob_ref, oc_ref, ga_ref, gb_ref, gc_ref, x_ref,
                  pa_ref, pb_ref, pc_ref, wo_ref, o_ref):
    m = (_sigmoid(ga_ref[...]) * _dot(oa_ref[...], pa_ref[...])
         + _sigmoid(gb_ref[...]) * _dot(ob_ref[...], pb_ref[...])
         + _sigmoid(gc_ref[...]) * _dot(oc_ref[...], pc_ref[...]))
    o_ref[...] = x_ref[...] + _dot(m, wo_ref[...])


def _merge(oa, ob, oc, p, x, pa, pb, pc, wo, *, tm):
    n, d = x.shape
    w = oa.shape[1]
    rows = lambda width, c: pl.BlockSpec((tm, width), lambda i: (i, c))
    const = lambda a: pl.BlockSpec(a.shape, lambda i: (0, 0))
    return pl.pallas_call(
        _merge_kernel,
        name="merge_out_proj",
        grid=(n // tm,),
        in_specs=[rows(w, 0), rows(w, 0), rows(w, 0),
                  rows(d, COL_GA // d), rows(d, COL_GB // d), rows(d, COL_GC // d),
                  rows(d, 0), const(pa), const(pb), const(pc), const(wo)],
        out_specs=rows(d, 0),
        out_shape=jax.ShapeDtypeStruct((n, d), F32),
        compiler_params=_cparams(("parallel",)),
    )(oa, ob, oc, p, p, p, x, pa, pb, pc, wo)


def _ffn_kernel(x_ref, prev_ref, g_ref, wup_ref, cw_ref, cb_ref, wdn_ref, o_ref, conv_ref,
                carry_sc, *, tm, fc):
    t = pl.program_id(1)

    @pl.when(t == 0)
    def _():
        carry_sc[...] = prev_ref[0]

    x = x_ref[0]
    xn = _rms(x, g_ref[...], NORM_EPS).astype(MXU_DTYPE)
    row = lax.broadcasted_iota(jnp.int32, (tm, fc), 0)
    acc = x
    for c in range(D_FF // fc):
        cs = slice(c * fc, (c + 1) * fc)
        up = jnp.dot(xn, wup_ref[:, cs], preferred_element_type=F32)
        gate = jnp.dot(xn, wup_ref[:, D_FF + c * fc:D_FF + (c + 1) * fc],
                       preferred_element_type=F32)
        old = carry_sc[:, cs]
        r1 = pltpu.roll(up, 1, 0)
        r2 = pltpu.roll(up, 2, 0)
        up_m1 = jnp.where(row == 0, old[1:2], r1)
        up_m2 = jnp.where(row == 0, old[0:1], jnp.where(row == 1, old[1:2], r2))
        carry_sc[:, cs] = r2[0:2]
        conv = (cb_ref[:, cs] + cw_ref[0:1, cs] * up_m2 + cw_ref[1:2, cs] * up_m1
                + cw_ref[2:3, cs] * up)
        hid = 0.5 * conv * (1.0 + lax.erf(conv * (2.0 ** -0.5))) * gate
        acc = acc + jnp.dot(hid.astype(MXU_DTYPE), wdn_ref[cs, :], preferred_element_type=F32)
    o_ref[0] = acc

    @pl.when(t == pl.num_programs(1) - 1)
    def _():
        conv_ref[0] = carry_sc[...]


def _ffn(x3, prev, g, wup, cw, cb, wdn, *, tm, fc):
    nb, nt, d = x3.shape
    const = lambda a: pl.BlockSpec(a.shape, lambda b, t: (0, 0))
    st_spec = pl.BlockSpec((1, FFN_CONV - 1, D_FF), lambda b, t: (b, 0, 0))
    return pl.pallas_call(
        functools.partial(_ffn_kernel, tm=tm, fc=fc),
        name="conv_ffn",
        grid=(nb, nt // tm),
        in_specs=[pl.BlockSpec((1, tm, d), lambda b, t: (b, t, 0)), st_spec,
                  const(g), const(wup), const(cw), const(cb), const(wdn)],
        out_specs=[pl.BlockSpec((1, tm, d), lambda b, t: (b, t, 0)), st_spec],
        out_shape=[jax.ShapeDtypeStruct((nb, nt, d), F32),
                   jax.ShapeDtypeStruct((nb, FFN_CONV - 1, D_FF), F32)],
        scratch_shapes=[pltpu.VMEM((FFN_CONV - 1, D_FF), F32)],
        compiler_params=_cparams(("parallel", "arbitrary")),
    )(x3, prev, g, wup, cw, cb, wdn)


def _regroup_in_proj(w):
    d = w.shape[0]
    z = lambda n: jnp.zeros((d, n), w.dtype)
    return jnp.concatenate(
        [w[:, 5408:8480], w[:, 1536:3072], w[:, 0:1536], w[:, 3360:5408],
         w[:, 3200:3360], z(GD_W - B_GATE_LORA), w[:, 3072:3200], z(WP - COL_WA - WA_W)], axis=1)


def _pad_rows(w, rows, at):
    out = jnp.zeros((rows, w.shape[1]), w.dtype)
    return out.at[at:at + w.shape[0]].set(w)


def _row_tile(n, pref):
    t = min(n, pref)
    while n % t:
        t //= 2
    return t


def _layer_weights(l, W):
    row = lambda a: a.reshape(1, -1)
    mu = W["b_mu"][l]
    wts = {
        "mu_rkv": row(mu[0:RKV_W]),
        "mu_wa": row(mu[RKV_W:RKV_W + WA_W]),
        "mu_gd": row(jnp.pad(mu[RKV_W + WA_W:], (0, GD_W - B_GATE_LORA))),
        "w0": row(W["b_w0"][l]),
        "w2p": _pad_rows(W["b_w2"][l], WA_W, 0).astype(MXU_DTYPE),
        "a0": row(W["b_a0"][l]),
        "a2p": _pad_rows(W["b_a2"][l], WA_W, B_DECAY_LORA).astype(MXU_DTYPE),
        "g2p": _pad_rows(W["b_g2"][l], GD_W, 0).astype(MXU_DTYPE),
        "kk_w": row(W["b_k_k"][l]), "ka_w": row(W["b_k_a"][l]), "rk_w": row(W["b_r_k"][l]),
        "lnw": row(W["b_ln_w"][l]), "lnb": row(W["b_ln_b"][l]),
    }
    if l > 0:
        wts["v0"] = row(W["b_v0"][l - 1])
        wts["v1p"] = jnp.pad(W["b_v1"][l - 1], ((0, 0), (0, LANE - B_MV_LORA))).astype(MXU_DTYPE)
        wts["v2p"] = _pad_rows(W["b_v2"][l - 1], LANE, 0).astype(MXU_DTYPE)
    return wts


def _block(x3, l, W, w_in, v_first, shift_row, kv_cache, s_rwkv, s_hgrn, conv_prev, *, prompt):
    nb, nt, d = x3.shape
    n = nb * nt
    x2 = x3.reshape(n, d)
    tm = _row_tile(n, 512)
    mix_g = W["mix_norm"][l].reshape(1, d)
    p = _norm_matmul(x2, mix_g, w_in, apply_norm=True, tm=tm, tn=WP // 4)
    p3 = p.reshape(nb, nt, WP)

    lam_init = 0.8 - 0.6 * math.exp(-0.3 * l)
    lq, lk = W["a_lq"][l], W["a_lk"][l]
    subln = W["a_subln"][l].reshape(1, A_VDIM)
    if prompt:
        oa = _attn_prompt(p3, lq, lk, subln, lam_init=lam_init, tq=_row_tile(nt, 256))
    else:
        ck, cv = kv_cache
        past = ck.shape[1]
        oa = _attn_sample(p3, ck.reshape(nb, past, -1), cv.reshape(nb, past, -1), lq, lk, subln,
                          lam_init=lam_init)

    if shift_row is None:
        prev3 = jnp.zeros((nb, 1, WP), F32)
    else:
        prev3 = _norm_matmul(shift_row, mix_g, w_in, apply_norm=False,
                             tm=nb, tn=WP // 4).reshape(nb, 1, WP)
    L = min(CHUNK, nt)
    ob, v_first, s_rwkv = _rwkv(p3, prev3, s_rwkv, v_first, _layer_weights(l, W), L=L)
    oc, s_hgrn = _hgrn(p3, s_hgrn, W["c_lb"], W["c_norm"][l].reshape(1, C_DIM), L=L, layer=l)

    bf = lambda a: a.astype(MXU_DTYPE)
    x2 = _merge(oa.reshape(n, -1), ob.reshape(n, -1), oc.reshape(n, -1), p, x2,
                bf(W["proj_a"][l]), bf(W["proj_b"][l]), bf(W["proj_c"][l]), bf(W["out_proj"][l]),
                tm=tm)
    x3n, conv_new = _ffn(x2.reshape(nb, nt, d), conv_prev, W["ffn_norm"][l].reshape(1, d),
                         bf(W["ffn_up"][l]), W["ffn_conv_w"][l], W["ffn_conv_b"][l].reshape(1, -1),
                         bf(W["ffn_down"][l]), tm=_row_tile(nt, 256), fc=D_FF // 2)

    k_new = p3[:, :, COL_AK:COL_AK + 512].reshape(nb, nt, A_HEADS, 2, A_HEAD_DIM)
    v_new = p3[:, :, COL_AV:COL_AV + 512].reshape(nb, nt, A_HEADS, A_VDIM)
    shift_new = _rmsnorm(x3[:, -1], mix_g, tm=nb)
    return x3n, v_first, (k_new, v_new, shift_new, s_rwkv, s_hgrn, conv_new)


def kernel(x_prompt, x_sample, cache_attn_k, cache_attn_v, state_rwkv_shift, state_rwkv_wkv, state_hgrn, state_ffn_conv, mix_norm, in_proj, a_lq, a_lk, a_subln, b_mu, b_w0, b_w2, b_a0, b_a2, b_g2, b_k_k, b_k_a, b_r_k, b_ln_w, b_ln_b, b_v0, b_v1, b_v2, c_lb, c_norm, proj_a, proj_b, proj_c, out_proj, ffn_norm, ffn_up, ffn_conv_w, ffn_conv_b, ffn_down, final_norm):
    W = dict(mix_norm=mix_norm, in_proj=in_proj, a_lq=a_lq, a_lk=a_lk, a_subln=a_subln,
             b_mu=b_mu, b_w0=b_w0, b_w2=b_w2, b_a0=b_a0, b_a2=b_a2, b_g2=b_g2, b_k_k=b_k_k,
             b_k_a=b_k_a, b_r_k=b_r_k, b_ln_w=b_ln_w, b_ln_b=b_ln_b, b_v0=b_v0, b_v1=b_v1,
             b_v2=b_v2, c_lb=c_lb, c_norm=c_norm, proj_a=proj_a, proj_b=proj_b, proj_c=proj_c,
             out_proj=out_proj, ffn_norm=ffn_norm, ffn_up=ffn_up, ffn_conv_w=ffn_conv_w,
             ffn_conv_b=ffn_conv_b, ffn_down=ffn_down)
    depth = in_proj.shape[0]
    n_p = x_prompt.shape[0]
    xp, xs = x_prompt, x_sample
    vf_p = vf_s = None
    st_p, st_s = [], []
    for l in range(depth):
        w_in = _regroup_in_proj(in_proj[l]).astype(MXU_DTYPE)
        xp, vf_p, s_p = _block(
            xp, l, W, w_in, vf_p, None, None,
            jnp.zeros((n_p, B_HEADS, B_HEAD_DIM, B_HEAD_DIM), F32),
            jnp.zeros((n_p, C_HEADS, C_DIM, C_DIM), F32),
            jnp.zeros((n_p, FFN_CONV - 1, D_FF), F32), prompt=True)
        st_p.append(s_p)
        xs, vf_s, s_s = _block(
            xs, l, W, w_in, vf_s, state_rwkv_shift[l], (cache_attn_k[l], cache_attn_v[l]),
            state_rwkv_wkv[l], state_hgrn[l], state_ffn_conv[l], prompt=False)
        st_s.append(s_s)

    fin = final_norm.reshape(1, -1)
    d = xp.shape[-1]
    y_prompt = _rmsnorm(xp.reshape(-1, d), fin, tm=_row_tile(xp.shape[0] * xp.shape[1], 512))
    y_sample = _rmsnorm(xs.reshape(-1, d), fin, tm=_row_tile(xs.shape[0] * xs.shape[1], 512))
    stack = lambda states: [jnp.stack([s[i] for s in states]) for i in range(len(states[0]))]
    return (y_prompt.reshape(xp.shape), y_sample.reshape(xs.shape), *stack(st_p), *stack(st_s))
```

```python
import functools
import math

import numpy as np
import jax
import jax.numpy as jnp
from jax import lax
from jax.experimental import pallas as pl
from jax.experimental.pallas import tpu as pltpu

F32 = jnp.float32
MXU_DTYPE = jnp.bfloat16

NORM_EPS = 1e-6
NEG_BIG = -1e30
CHUNK = 64

A_HEADS = 4
A_HEAD_DIM = 64
A_VDIM = 128
A_SUBLN_EPS = 1e-5

B_HEADS = 8
B_HEAD_DIM = 64
B_W = 512
B_DECAY_LORA = 64
B_AAA_LORA = 64
B_MV_LORA = 32
B_GATE_LORA = 160
B_GN_EPS = 64e-5

C_HEADS = 4
C_DIM = 128
C_W = 512

D_FF = 2816
FFN_CONV = 3

LANE = 128
VMEM_LIMIT = 56 * 1024 * 1024

COL_GA, COL_GB, COL_GC = 0, 1024, 2048
COL_RKV = 3072
COL_AQ, COL_AK, COL_AV = 4608, 5120, 5632
COL_CQ, COL_CF, COL_CI, COL_COG = 6144, 6656, 7168, 7680
COL_GD = 8192
COL_WA = 8448
WP = 8704
GD_W = 256
WA_W = 128
RKV_W = 3 * B_W


def _cparams(sem):
    return pltpu.CompilerParams(dimension_semantics=sem, vmem_limit_bytes=VMEM_LIMIT)


def _sigmoid(x):
    return 1.0 / (1.0 + jnp.exp(-x))


def _rms(x, w, eps):
    return x * lax.rsqrt(jnp.mean(x * x, axis=-1, keepdims=True) + eps) * w


_NN = (((1,), (0,)), ((), ()))
_NT = (((1,), (1,)), ((), ()))
_TN = (((0,), (0,)), ((), ()))


def _dot(a, b, dims=_NN):
    return lax.dot_general(a.astype(MXU_DTYPE), b.astype(MXU_DTYPE), dims,
                           preferred_element_type=F32)


def _split2(a):
    hi = a.astype(MXU_DTYPE)
    lo = (a - hi.astype(F32)).astype(MXU_DTYPE)
    return hi, lo


def _dot3(a, b, dims=_NN):
    ah, al = _split2(a)
    bh, bl = _split2(b)
    d = lambda x, y: lax.dot_general(x, y, dims, preferred_element_type=F32)
    return d(ah, bh) + (d(ah, bl) + d(al, bh))


_RWKV_DOT = _dot


def _norm_matmul_kernel(x_ref, g_ref, w_ref, o_ref, xn_ref, *, apply_norm):
    @pl.when(pl.program_id(1) == 0)
    def _():
        x = x_ref[...]
        if apply_norm:
            x = _rms(x, g_ref[...], NORM_EPS)
        xn_ref[...] = x.astype(xn_ref.dtype)

    o_ref[...] = jnp.dot(xn_ref[...], w_ref[...], preferred_element_type=F32)


def _norm_matmul(x, g, w, *, apply_norm, tm, tn):
    n, d = x.shape
    wn = w.shape[1]
    return pl.pallas_call(
        functools.partial(_norm_matmul_kernel, apply_norm=apply_norm),
        name="norm_matmul",
        grid=(n // tm, wn // tn),
        in_specs=[pl.BlockSpec((tm, d), lambda i, j: (i, 0)),
                  pl.BlockSpec((1, d), lambda i, j: (0, 0)),
                  pl.BlockSpec((d, tn), lambda i, j: (0, j))],
        out_specs=pl.BlockSpec((tm, tn), lambda i, j: (i, j)),
        out_shape=jax.ShapeDtypeStruct((n, wn), F32),
        scratch_shapes=[pltpu.VMEM((tm, d), w.dtype)],
        compiler_params=_cparams(("parallel", "arbitrary")),
    )(x, g, w)


def _rmsnorm_kernel(x_ref, g_ref, o_ref):
    o_ref[...] = _rms(x_ref[...], g_ref[...], NORM_EPS)


def _rmsnorm(x, g, *, tm):
    n, d = x.shape
    return pl.pallas_call(
        _rmsnorm_kernel,
        name="rmsnorm",
        grid=(n // tm,),
        in_specs=[pl.BlockSpec((tm, d), lambda i: (i, 0)),
                  pl.BlockSpec((1, d), lambda i: (0, 0))],
        out_specs=pl.BlockSpec((tm, d), lambda i: (i, 0)),
        out_shape=jax.ShapeDtypeStruct((n, d), F32),
        compiler_params=_cparams(("parallel",)),
    )(x, g)


def _map_queries(q, scale=A_HEAD_DIM ** -0.5):
    q = q * scale
    first = lax.broadcasted_iota(jnp.int32, q.shape, 1) < A_HEAD_DIM
    return (jnp.where(first, q, 0.0).astype(MXU_DTYPE),
            jnp.where(first, 0.0, q).astype(MXU_DTYPE))


def _diff_lambda(lq_ref, lk_ref, lam_init):
    e = jnp.exp(jnp.sum(lq_ref[...] * lk_ref[...], axis=-1, keepdims=True))
    return e[0:1] - e[1:2] + lam_init


def _attn_finish(o1, o2, lam, subln, lam_init):
    o = o1 - lam * o2
    return _rms(o, subln, A_SUBLN_EPS) * (1.0 - lam_init)


def _attn_prompt_kernel(q_ref, k_ref, v_ref, lq_ref, lk_ref, sub_ref, o_ref, *, tq, tk, lam_init):
    qi = pl.program_id(2)
    qs = _map_queries(q_ref[0], A_HEAD_DIM ** -0.5 * math.log2(math.e))
    log2_chunk = int(round(math.log2(CHUNK)))
    q_chunk = (qi * tq + lax.broadcasted_iota(jnp.int32, (tq, tk), 0)) >> log2_chunk
    ones = jnp.ones((tk, A_VDIM), MXU_DTYPE)

    def block(kstart, carry, masked):
        k = k_ref[0, pl.ds(kstart, tk), :].astype(MXU_DTYPE)
        v1 = jnp.concatenate([v_ref[0, pl.ds(kstart, tk), :].astype(MXU_DTYPE), ones], axis=1)
        if masked:
            k_chunk = (kstart + lax.broadcasted_iota(jnp.int32, (tq, tk), 1)) >> log2_chunk
            visible = k_chunk <= q_chunk
        new = []
        for c in range(2):
            m_prev, acc_prev = carry[c]
            s = lax.dot_general(qs[c], k, _NT, preferred_element_type=F32)
            if masked:
                s = jnp.where(visible, s, NEG_BIG)
            m_new = jnp.maximum(m_prev, jnp.max(s, axis=-1, keepdims=True))
            alpha = jnp.exp2(m_prev - m_new)
            p = jnp.exp2(s - m_new)
            acc_new = alpha * acc_prev + jnp.dot(p.astype(MXU_DTYPE), v1,
                                                 preferred_element_type=F32)
            new.append((m_new, acc_new))
        return tuple(new)

    init = tuple((jnp.full((tq, 1), NEG_BIG, F32), jnp.zeros((tq, 2 * A_VDIM), F32))
                 for _ in range(2))
    n_full = qi >> int(round(math.log2(tk // tq)))
    carry = lax.fori_loop(
        0, n_full, lambda i, c: block(pl.multiple_of(i * tk, tk), c, False), init)
    (_, acc1), (_, acc2) = block(pl.multiple_of(n_full * tk, tk), carry, True)
    lam = _diff_lambda(lq_ref, lk_ref, lam_init)
    o_ref[0] = _attn_finish(acc1[:, :A_VDIM] / acc1[:, A_VDIM:], acc2[:, :A_VDIM] / acc2[:, A_VDIM:],
                            lam, sub_ref[...], lam_init)


def _attn_prompt(p3, lq, lk, subln, *, lam_init, tq, tk):
    nb, nt, _ = p3.shape
    cq, ck, cv = COL_AQ // LANE, COL_AK // LANE, COL_AV // LANE
    const = lambda shape: pl.BlockSpec(shape, lambda b, h, qi: (0, 0))
    return pl.pallas_call(
        functools.partial(_attn_prompt_kernel, tq=tq, tk=tk, lam_init=lam_init),
        name="attn_prompt",
        grid=(nb, A_HEADS, nt // tq),
        in_specs=[
            pl.BlockSpec((1, tq, LANE), lambda b, h, qi: (b, qi, cq + h)),
            pl.BlockSpec((1, nt, LANE), lambda b, h, qi: (b, 0, ck + h)),
            pl.BlockSpec((1, nt, LANE), lambda b, h, qi: (b, 0, cv + h)),
            const((2, A_HEAD_DIM)), const((2, A_HEAD_DIM)), const((1, A_VDIM)),
        ],
        out_specs=pl.BlockSpec((1, tq, LANE), lambda b, h, qi: (b, qi, h)),
        out_shape=jax.ShapeDtypeStruct((nb, nt, A_HEADS * A_VDIM), F32),
        compiler_params=_cparams(("parallel", "parallel", "arbitrary")),
    )(p3, p3, p3, lq, lk, subln)


def _attn_sample_kernel(q_ref, kn_ref, vn_ref, kc_ref, vc_ref, lq_ref, lk_ref, sub_ref, o_ref,
                        *, lam_init):
    qs = _map_queries(q_ref[0])
    kn = kn_ref[0].astype(MXU_DTYPE)
    vn = vn_ref[0].astype(MXU_DTYPE)
    kc = kc_ref[0].astype(MXU_DTYPE)
    vc = vc_ref[0].astype(MXU_DTYPE)
    outs = []
    for c in range(2):
        s_c = lax.dot_general(qs[c], kc, _NT, preferred_element_type=F32)
        s_n = lax.dot_general(qs[c], kn, _NT, preferred_element_type=F32)
        m = jnp.maximum(jnp.max(s_c, axis=-1, keepdims=True), jnp.max(s_n, axis=-1, keepdims=True))
        p_c = jnp.exp(s_c - m)
        p_n = jnp.exp(s_n - m)
        l = jnp.sum(p_c, axis=-1, keepdims=True) + jnp.sum(p_n, axis=-1, keepdims=True)
        acc = (jnp.dot(p_c.astype(MXU_DTYPE), vc, preferred_element_type=F32)
               + jnp.dot(p_n.astype(MXU_DTYPE), vn, preferred_element_type=F32))
        outs.append(acc / l)
    lam = _diff_lambda(lq_ref, lk_ref, lam_init)
    o_ref[0] = _attn_finish(outs[0], outs[1], lam, sub_ref[...], lam_init)


def _attn_sample(p3, cache_k, cache_v, lq, lk, subln, *, lam_init):
    nb, nt, _ = p3.shape
    past = cache_k.shape[1]
    cq, ck, cv = COL_AQ // LANE, COL_AK // LANE, COL_AV // LANE
    return pl.pallas_call(
        functools.partial(_attn_sample_kernel, lam_init=lam_init),
        name="attn_sample",
        grid=(nb, A_HEADS),
        in_specs=[
            pl.BlockSpec((1, nt, LANE), lambda b, h: (b, 0, cq + h)),
            pl.BlockSpec((1, nt, LANE), lambda b, h: (b, 0, ck + h)),
            pl.BlockSpec((1, nt, LANE), lambda b, h: (b, 0, cv + h)),
            pl.BlockSpec((1, past, LANE), lambda b, h: (b, 0, h)),
            pl.BlockSpec((1, past, LANE), lambda b, h: (b, 0, h)),
            pl.BlockSpec((2, A_HEAD_DIM), lambda b, h: (0, 0)),
            pl.BlockSpec((2, A_HEAD_DIM), lambda b, h: (0, 0)),
            pl.BlockSpec((1, A_VDIM), lambda b, h: (0, 0)),
        ],
        out_specs=pl.BlockSpec((1, nt, LANE), lambda b, h: (b, 0, h)),
        out_shape=jax.ShapeDtypeStruct((nb, nt, A_HEADS * A_VDIM), F32),
        compiler_params=_cparams(("parallel", "parallel")),
    )(p3, p3, p3, cache_k, cache_v, lq, lk, subln)


def _cumsum_rows(x):
    n = x.shape[0]
    row = lax.broadcasted_iota(jnp.int32, x.shape, 0)
    s = 1
    while s < n:
        x = x + jnp.where(row >= s, pltpu.roll(x, s, 0), 0.0)
        s *= 2
    return x


def _shift_lerp(x, carry, mu):
    row = lax.broadcasted_iota(jnp.int32, x.shape, 0)
    prev = jnp.where(row == 0, carry, pltpu.roll(x, 1, 0))
    return x + (prev - x) * mu


def _rwkv_kernel(*refs, L, vmix):
    it = iter(refs)
    rkv_ref, gd_ref, wa_ref = next(it), next(it), next(it)
    p_rkv_ref, p_gd_ref, p_wa_ref = next(it), next(it), next(it)
    s0_ref = next(it)
    vf_ref = next(it) if vmix else None
    mu_rkv, mu_gd, mu_wa = next(it), next(it), next(it)
    w0, w2p, a0, a2p, g2p = next(it), next(it), next(it), next(it), next(it)
    kk_w, ka_w, rk_w, lnw, lnb = next(it), next(it), next(it), next(it), next(it)
    if vmix:
        v0, v1p, v2p = next(it), next(it), next(it)
    o_ref = next(it)
    vout_ref = None if vmix else next(it)
    sout_ref = next(it)
    s_sc, c_rkv, c_gd, c_wa = next(it), next(it), next(it), next(it)

    t = pl.program_id(1)

    @pl.when(t == 0)
    def _():
        s_sc[...] = s0_ref[0]
        c_rkv[...] = p_rkv_ref[0]
        c_gd[...] = p_gd_ref[0]
        c_wa[...] = p_wa_ref[0]

    x_rkv, x_gd, x_wa = rkv_ref[0], gd_ref[0], wa_ref[0]
    xs_rkv = _shift_lerp(x_rkv, c_rkv[...], mu_rkv[...])
    xs_gd = _shift_lerp(x_gd, c_gd[...], mu_gd[...])
    xs_wa = _shift_lerp(x_wa, c_wa[...], mu_wa[...])
    c_rkv[...] = x_rkv[L - 1:L]
    c_gd[...] = x_gd[L - 1:L]
    c_wa[...] = x_wa[L - 1:L]

    r = xs_rkv[:, 0:B_W]
    k = xs_rkv[:, B_W:2 * B_W]
    v = xs_rkv[:, 2 * B_W:3 * B_W]

    zw = -(w0[...] + _dot(jnp.tanh(xs_wa), w2p[...]))
    softplus = jnp.maximum(zw, 0.0) + jnp.log(1.0 + jnp.exp(-jnp.abs(zw)))
    lw = -jnp.exp(-softplus - 0.5)
    a = _sigmoid(a0[...] + _dot(xs_wa, a2p[...]))
    g = _dot(_sigmoid(xs_gd), g2p[...])
    if vmix:
        mix = _sigmoid(v0[...] + _dot(_dot(v, v1p[...]), v2p[...]))
        v = v + (vf_ref[0] - v) * mix
    else:
        vout_ref[0] = v

    kk_raw = k * kk_w[...]
    kp = k * (1.0 + (a - 1.0) * ka_w[...])

    G = _cumsum_rows(lw)
    GL = G[L - 1:L]
    e_g = jnp.exp(G)
    e_gx = jnp.exp(G - lw)
    e_ng = jnp.exp(-G)
    e_glg = jnp.exp(GL - G)
    e_gl = jnp.exp(GL)

    row = lax.broadcasted_iota(jnp.int32, (L, L), 0)
    col = lax.broadcasted_iota(jnp.int32, (L, L), 1)
    strict = row > col
    eye = jnp.where(row == col, 1.0, 0.0)
    row2 = lax.broadcasted_iota(jnp.int32, (L, 2 * L), 0)
    col2 = lax.broadcasted_iota(jnp.int32, (L, 2 * L), 1)
    strict_k = (col2 >= L) & (row2 > col2 - L)
    incl2 = row2 >= (col2 & (L - 1))
    n_sq = int(round(math.log2(L))) - 1
    heads = range(B_HEADS)
    hs = [slice(h * B_HEAD_DIM, (h + 1) * B_HEAD_DIM) for h in heads]

    r_h = [r[:, s] for s in hs]
    v_h = [v[:, s] for s in hs]
    kp_h = [kp[:, s] for s in hs]
    kk_h = []
    for s in hs:
        t_kk = kk_raw[:, s]
        kk_h.append(t_kk / jnp.maximum(
            jnp.sqrt(jnp.sum(t_kk * t_kk, axis=-1, keepdims=True)), 1e-12))
    beta = [-(kk_h[h] * a[:, hs[h]]) for h in heads]
    lhs = [jnp.concatenate([kk_h[h] * e_gx[:, hs[h]], r_h[h] * e_g[:, hs[h]]], axis=0)
           for h in heads]
    rhs = [jnp.concatenate([beta[h] * e_ng[:, hs[h]], kp_h[h] * e_ng[:, hs[h]]], axis=0)
           for h in heads]
    m = [_RWKV_DOT(lhs[h], rhs[h], _NT) for h in heads]
    n_pow = [jnp.where(strict, m[h][0:L, 0:L], 0.0) for h in heads]
    top_k = [jnp.where(strict_k, m[h][0:L], 0.0) for h in heads]
    bot = [jnp.where(incl2, m[h][L:2 * L], 0.0) for h in heads]

    tinv = [eye + n_pow[h] for h in heads]
    for _ in range(n_sq):
        n_pow = [_RWKV_DOT(n_pow[h], n_pow[h]) for h in heads]
        tinv = [tinv[h] + _RWKV_DOT(tinv[h], n_pow[h]) for h in heads]

    s0 = [s_sc[h] for h in heads]
    ls = [_RWKV_DOT(lhs[h], s0[h], _NT) for h in heads]
    vv = [jnp.concatenate([v_h[h], v_h[h]], axis=0) for h in heads]
    x = [ls[h][0:L] + _RWKV_DOT(top_k[h], vv[h]) for h in heads]
    u = [_RWKV_DOT(tinv[h], x[h]) for h in heads]
    uv = [jnp.concatenate([u[h], v_h[h]], axis=0) for h in heads]
    y = [ls[h][L:2 * L] + _RWKV_DOT(bot[h], uv[h]) for h in heads]
    bk = [jnp.concatenate([beta[h] * e_glg[:, hs[h]], kp_h[h] * e_glg[:, hs[h]]], axis=0)
          for h in heads]
    for h in heads:
        s_sc[h] = s0[h] * e_gl[:, hs[h]] + _RWKV_DOT(uv[h], bk[h], _TN)

    for h in heads:
        sl = hs[h]
        mean = jnp.mean(y[h], axis=-1, keepdims=True)
        var = jnp.mean(jnp.square(y[h] - mean), axis=-1, keepdims=True)
        yn = (y[h] - mean) * lax.rsqrt(var + B_GN_EPS) * lnw[:, sl] + lnb[:, sl]
        yn = yn + jnp.sum(r_h[h] * kp_h[h] * rk_w[:, sl], axis=-1, keepdims=True) * v_h[h]
        o_ref[0, :, sl] = yn * g[:, sl]

    @pl.when(t == pl.num_programs(1) - 1)
    def _():
        sout_ref[0] = s_sc[...]


def _rwkv(p3, prev3, s0, v_first, wts, *, L):
    nb, nt, _ = p3.shape
    vmix = v_first is not None
    c_rkv, c_gd, c_wa = COL_RKV // RKV_W, COL_GD // GD_W, COL_WA // WA_W
    const = lambda shape: pl.BlockSpec(shape, lambda b, t: (0,) * len(shape))
    in_specs = [
        pl.BlockSpec((1, L, RKV_W), lambda b, t: (b, t, c_rkv)),
        pl.BlockSpec((1, L, GD_W), lambda b, t: (b, t, c_gd)),
        pl.BlockSpec((1, L, WA_W), lambda b, t: (b, t, c_wa)),
        pl.BlockSpec((1, 1, RKV_W), lambda b, t: (b, 0, c_rkv)),
        pl.BlockSpec((1, 1, GD_W), lambda b, t: (b, 0, c_gd)),
        pl.BlockSpec((1, 1, WA_W), lambda b, t: (b, 0, c_wa)),
        pl.BlockSpec((1, B_HEADS, B_HEAD_DIM, B_HEAD_DIM), lambda b, t: (b, 0, 0, 0)),
    ]
    args = [p3, p3, p3, prev3, prev3, prev3, s0]
    if vmix:
        in_specs.append(pl.BlockSpec((1, L, B_W), lambda b, t: (b, t, 0)))
        args.append(v_first)
    names = ["mu_rkv", "mu_gd", "mu_wa", "w0", "w2p", "a0", "a2p", "g2p",
             "kk_w", "ka_w", "rk_w", "lnw", "lnb"]
    if vmix:
        names += ["v0", "v1p", "v2p"]
    for nme in names:
        in_specs.append(const(wts[nme].shape))
        args.append(wts[nme])
    seq_spec = pl.BlockSpec((1, L, B_W), lambda b, t: (b, t, 0))
    seq_shape = jax.ShapeDtypeStruct((nb, nt, B_W), F32)
    st_spec = pl.BlockSpec((1, B_HEADS, B_HEAD_DIM, B_HEAD_DIM), lambda b, t: (b, 0, 0, 0))
    st_shape = jax.ShapeDtypeStruct((nb, B_HEADS, B_HEAD_DIM, B_HEAD_DIM), F32)
    if vmix:
        out_specs, out_shape = [seq_spec, st_spec], [seq_shape, st_shape]
    else:
        out_specs, out_shape = [seq_spec, seq_spec, st_spec], [seq_shape, seq_shape, st_shape]
    outs = pl.pallas_call(
        functools.partial(_rwkv_kernel, L=L, vmix=vmix),
        name="rwkv7",
        grid=(nb, nt // L),
        in_specs=in_specs, out_specs=out_specs, out_shape=out_shape,
        scratch_shapes=[pltpu.VMEM((B_HEADS, B_HEAD_DIM, B_HEAD_DIM), F32),
                        pltpu.VMEM((1, RKV_W), F32), pltpu.VMEM((1, GD_W), F32),
                        pltpu.VMEM((1, WA_W), F32)],
        compiler_params=_cparams(("parallel", "arbitrary")),
    )(*args)
    if vmix:
        return outs[0], v_first, outs[1]
    return outs[0], outs[1], outs[2]


def _hgrn_levels(L):
    return [L >> (j + 1) for j in range(int(round(math.log2(L))))]


def _hgrn_sum_masks(L):
    t = np.arange(L)[:, None]
    i = np.arange(L)[None, :]
    blocks = []
    for b in _hgrn_levels(L):
        mid = (t // (2 * b)) * (2 * b) + b - 1
        upper = (t % (2 * b)) >= b
        blocks.append(upper & (i > mid) & (i <= t))
        blocks.append((~upper) & (i > t) & (i <= mid))
    blocks.append(i <= t)
    blocks.append(i > t)
    return np.concatenate(blocks, axis=0).astype(np.float32)


def _hgrn_kernel(cq_ref, cf_ref, ci_ref, cog_ref, s0_ref, clb_ref, cnorm_ref, mm_ref,
                 o_ref, sout_ref, st_sc, *, L, layer):
    t = pl.program_id(1)

    @pl.when(t == 0)
    def _():
        for h in range(C_HEADS):
            st_sc[h] = s0_ref[0, h].T

    clb = clb_ref[...]
    e = jnp.exp(clb - jnp.max(clb, axis=0, keepdims=True))
    soft = e / jnp.sum(e, axis=0, keepdims=True)
    lb = jnp.zeros((1, C_W), F32)
    for i in range(1, layer + 1):
        lb = lb + soft[i:i + 1]

    z, q, v = cf_ref[0], cq_ref[0], ci_ref[0]
    f = lb + (1.0 - lb) * _sigmoid(z)
    log_f = jnp.log(f)
    k = (1.0 - lb) * _sigmoid(-z)

    hi = log_f.astype(MXU_DTYPE)
    r1 = log_f - hi.astype(F32)
    mid = r1.astype(MXU_DTYPE)
    lo = (r1 - mid.astype(F32)).astype(MXU_DTYPE)
    mm = mm_ref[...]
    d = lambda x: jnp.dot(mm, x, preferred_element_type=F32)
    sums = d(hi) + (d(mid) + d(lo))

    levels = _hgrn_levels(L)
    nl = len(levels)
    g_incl = sums[2 * nl * L:(2 * nl + 1) * L]
    g_suf = sums[(2 * nl + 1) * L:(2 * nl + 2) * L]
    g_last = g_incl[L - 1:L]
    q_in = q * jnp.exp(g_incl)
    k_out = k * jnp.exp(g_suf)
    e_last = jnp.exp(g_last)

    rowv = lax.broadcasted_iota(jnp.int32, (L, 1), 0)
    row = lax.broadcasted_iota(jnp.int32, (L, L), 0)
    col = lax.broadcasted_iota(jnp.int32, (L, L), 1)
    qt, kt, lmask = [], [], []
    for j, b in enumerate(levels):
        upper = (rowv & b) != 0
        qt.append(jnp.where(upper, q * jnp.exp(sums[2 * j * L:(2 * j + 1) * L]), 0.0))
        kt.append(jnp.where(upper, 0.0, k * jnp.exp(sums[(2 * j + 1) * L:(2 * j + 2) * L])))
        sh = int(round(math.log2(2 * b)))
        lmask.append(((row >> sh) == (col >> sh)) & ((row & b) != 0) & ((col & b) == 0))
    diag = row == col

    gate_in = cog_ref[0]
    heads = range(C_HEADS)
    hs = [slice(h * C_DIM, (h + 1) * C_DIM) for h in heads]
    a_mat = [jnp.where(diag, jnp.sum(q[:, s] * k[:, s], axis=-1, keepdims=True), 0.0) for s in hs]
    for j in range(nl):
        prod = [_dot(qt[j][:, s], kt[j][:, s], _NT) for s in hs]
        a_mat = [a_mat[h] + jnp.where(lmask[j], prod[h], 0.0) for h in heads]
    st = [st_sc[h] for h in heads]
    o = [_dot(a_mat[h], v[:, hs[h]]) + _dot(q_in[:, hs[h]], st[h], _NT) for h in heads]
    for h in heads:
        st_sc[h] = st[h] * e_last[:, hs[h]] + _dot(v[:, hs[h]], k_out[:, hs[h]], _TN)
    for h in heads:
        gate = gate_in[:, hs[h]]
        o_ref[0, :, hs[h]] = _rms(o[h], cnorm_ref[...], NORM_EPS) * (gate * _sigmoid(gate))

    @pl.when(t == pl.num_programs(1) - 1)
    def _():
        for h in range(C_HEADS):
            sout_ref[0, h] = st_sc[h].T


def _hgrn(p3, s0, c_lb, c_norm, *, L, layer):
    nb, nt, _ = p3.shape
    mm = jnp.asarray(_hgrn_sum_masks(L), MXU_DTYPE)
    cols = [COL_CQ // C_W, COL_CF // C_W, COL_CI // C_W, COL_COG // C_W]
    seq = lambda c: pl.BlockSpec((1, L, C_W), lambda b, t: (b, t, c))
    st_spec = pl.BlockSpec((1, C_HEADS, C_DIM, C_DIM), lambda b, t: (b, 0, 0, 0))
    return pl.pallas_call(
        functools.partial(_hgrn_kernel, L=L, layer=layer),
        name="hgrn2",
        grid=(nb, nt // L),
        in_specs=[seq(cols[0]), seq(cols[1]), seq(cols[2]), seq(cols[3]), st_spec,
                  pl.BlockSpec(c_lb.shape, lambda b, t: (0, 0)),
                  pl.BlockSpec((1, C_DIM), lambda b, t: (0, 0)),
                  pl.BlockSpec(mm.shape, lambda b, t: (0, 0))],
        out_specs=[pl.BlockSpec((1, L, C_W), lambda b, t: (b, t, 0)), st_spec],
        out_shape=[jax.ShapeDtypeStruct((nb, nt, C_W), F32),
                   jax.ShapeDtypeStruct((nb, C_HEADS, C_DIM, C_DIM), F32)],
        scratch_shapes=[pltpu.VMEM((C_HEADS, C_DIM, C_DIM), F32)],
        compiler_params=_cparams(("parallel", "arbitrary")),
    )(p3, p3, p3, p3, s0, c_lb, c_norm, mm)


def _merge_kernel(oa_ref, ob_ref, oc_ref, ga_ref, gb_ref, gc_ref, x_ref,
                  pa_ref, pb_ref, pc_ref, wo_ref, o_ref):
    m = (_sigmoid(ga_ref[...]) * _dot(oa_ref[...], pa_ref[...])
         + _sigmoid(gb_ref[...]) * _dot(ob_ref[...], pb_ref[...])
         + _sigmoid(gc_ref[...]) * _dot(oc_ref[...], pc_ref[...]))
    o_ref[...] = x_ref[...] + _dot(m, wo_ref[...])


def _merge(oa, ob, oc, p, x, pa, pb, pc, wo, *, tm):
    n, d = x.shape
    w = oa.shape[1]
    rows = lambda width, c: pl.BlockSpec((tm, width), lambda i: (i, c))
    const = lambda a: pl.BlockSpec(a.shape, lambda i: (0, 0))
    return pl.pallas_call(
        _merge_kernel,
        name="merge_out_proj",
        grid=(n // tm,),
        in_specs=[rows(w, 0), rows(w, 0), rows(w, 0),
                  rows(d, COL_GA // d), rows(d, COL_GB // d), rows(d, COL_GC // d),
                  rows(d, 0), const(pa), const(pb), const(pc), const(wo)],
        out_specs=rows(d, 0),
        out_shape=jax.ShapeDtypeStruct((n, d), F32),
        compiler_params=_cparams(("parallel",)),
    )(oa, ob, oc, p, p, p, x, pa, pb, pc, wo)


def _ffn_kernel(x_ref, prev_ref, g_ref, wup_ref, cw_ref, cb_ref, wdn_ref, o_ref, conv_ref,
                carry_sc, *, tm, fc):
    t = pl.program_id(1)

    @pl.when(t == 0)
    def _():
        carry_sc[...] = prev_ref[0]

    x = x_ref[0]
    xn = _rms(x, g_ref[...], NORM_EPS).astype(MXU_DTYPE)
    row = lax.broadcasted_iota(jnp.int32, (tm, fc), 0)
    acc = x
    for c in range(D_FF // fc):
        cs = slice(c * fc, (c + 1) * fc)
        up = jnp.dot(xn, wup_ref[:, cs], preferred_element_type=F32)
        gate = jnp.dot(xn, wup_ref[:, D_FF + c * fc:D_FF + (c + 1) * fc],
                       preferred_element_type=F32)
        old = carry_sc[:, cs]
        r1 = pltpu.roll(up, 1, 0)
        r2 = pltpu.roll(up, 2, 0)
        up_m1 = jnp.where(row == 0, old[1:2], r1)
        up_m2 = jnp.where(row == 0, old[0:1], jnp.where(row == 1, old[1:2], r2))
        carry_sc[:, cs] = r2[0:2]
        conv = (cb_ref[:, cs] + cw_ref[0:1, cs] * up_m2 + cw_ref[1:2, cs] * up_m1
                + cw_ref[2:3, cs] * up)
        hid = 0.5 * conv * (1.0 + lax.erf(conv * (2.0 ** -0.5))) * gate
        acc = acc + jnp.dot(hid.astype(MXU_DTYPE), wdn_ref[cs, :], preferred_element_type=F32)
    o_ref[0] = acc

    @pl.when(t == pl.num_programs(1) - 1)
    def _():
        conv_ref[0] = carry_sc[...]


def _ffn(x3, prev, g, wup, cw, cb, wdn, *, tm, fc):
    nb, nt, d = x3.shape
    const = lambda a: pl.BlockSpec(a.shape, lambda b, t: (0, 0))
    st_spec = pl.BlockSpec((1, FFN_CONV - 1, D_FF), lambda b, t: (b, 0, 0))
    return pl.pallas_call(
        functools.partial(_ffn_kernel, tm=tm, fc=fc),
        name="conv_ffn",
        grid=(nb, nt // tm),
        in_specs=[pl.BlockSpec((1, tm, d), lambda b, t: (b, t, 0)), st_spec,
                  const(g), const(wup), const(cw), const(cb), const(wdn)],
        out_specs=[pl.BlockSpec((1, tm, d), lambda b, t: (b, t, 0)), st_spec],
        out_shape=[jax.ShapeDtypeStruct((nb, nt, d), F32),
                   jax.ShapeDtypeStruct((nb, FFN_CONV - 1, D_FF), F32)],
        scratch_shapes=[pltpu.VMEM((FFN_CONV - 1, D_FF), F32)],
        compiler_params=_cparams(("parallel", "arbitrary")),
    )(x3, prev, g, wup, cw, cb, wdn)


def _regroup_in_proj(w):
    d = w.shape[0]
    z = lambda n: jnp.zeros((d, n), w.dtype)
    return jnp.concatenate(
        [w[:, 5408:8480], w[:, 1536:3072], w[:, 0:1536], w[:, 3360:5408],
         w[:, 3200:3360], z(GD_W - B_GATE_LORA), w[:, 3072:3200], z(WP - COL_WA - WA_W)], axis=1)


def _pad_rows(w, rows, at):
    out = jnp.zeros((rows, w.shape[1]), w.dtype)
    return out.at[at:at + w.shape[0]].set(w)


def _row_tile(n, pref):
    t = min(n, pref)
    while n % t:
        t //= 2
    return t


def _layer_weights(l, W):
    row = lambda a: a.reshape(1, -1)
    mu = W["b_mu"][l]
    wts = {
        "mu_rkv": row(mu[0:RKV_W]),
        "mu_wa": row(mu[RKV_W:RKV_W + WA_W]),
        "mu_gd": row(jnp.pad(mu[RKV_W + WA_W:], (0, GD_W - B_GATE_LORA))),
        "w0": row(W["b_w0"][l]),
        "w2p": _pad_rows(W["b_w2"][l], WA_W, 0).astype(MXU_DTYPE),
        "a0": row(W["b_a0"][l]),
        "a2p": _pad_rows(W["b_a2"][l], WA_W, B_DECAY_LORA).astype(MXU_DTYPE),
        "g2p": _pad_rows(W["b_g2"][l], GD_W, 0).astype(MXU_DTYPE),
        "kk_w": row(W["b_k_k"][l]), "ka_w": row(W["b_k_a"][l]), "rk_w": row(W["b_r_k"][l]),
        "lnw": row(W["b_ln_w"][l]), "lnb": row(W["b_ln_b"][l]),
    }
    if l > 0:
        wts["v0"] = row(W["b_v0"][l - 1])
        wts["v1p"] = jnp.pad(W["b_v1"][l - 1], ((0, 0), (0, LANE - B_MV_LORA))).astype(MXU_DTYPE)
        wts["v2p"] = _pad_rows(W["b_v2"][l - 1], LANE, 0).astype(MXU_DTYPE)
    return wts


def _block(x3, l, W, w_in, v_first, shift_row, kv_cache, s_rwkv, s_hgrn, conv_prev, *, prompt):
    nb, nt, d = x3.shape
    n = nb * nt
    x2 = x3.reshape(n, d)
    tm = _row_tile(n, 512)
    mix_g = W["mix_norm"][l].reshape(1, d)
    p = _norm_matmul(x2, mix_g, w_in, apply_norm=True, tm=tm, tn=WP // 4)
    p3 = p.reshape(nb, nt, WP)

    lam_init = 0.8 - 0.6 * math.exp(-0.3 * l)
    lq, lk = W["a_lq"][l], W["a_lk"][l]
    subln = W["a_subln"][l].reshape(1, A_VDIM)
    if prompt:
        oa = _attn_prompt(p3, lq, lk, subln, lam_init=lam_init, tq=_row_tile(nt, 256),
                          tk=_row_tile(nt, 1024))
    else:
        ck, cv = kv_cache
        past = ck.shape[1]
        oa = _attn_sample(p3, ck.reshape(nb, past, -1), cv.reshape(nb, past, -1), lq, lk, subln,
                          lam_init=lam_init)

    if shift_row is None:
        prev3 = jnp.zeros((nb, 1, WP), F32)
    else:
        prev3 = _norm_matmul(shift_row, mix_g, w_in, apply_norm=False,
                             tm=nb, tn=WP // 4).reshape(nb, 1, WP)
    L = min(CHUNK, nt)
    ob, v_first, s_rwkv = _rwkv(p3, prev3, s_rwkv, v_first, _layer_weights(l, W), L=L)
    oc, s_hgrn = _hgrn(p3, s_hgrn, W["c_lb"], W["c_norm"][l].reshape(1, C_DIM), L=L, layer=l)

    bf = lambda a: a.astype(MXU_DTYPE)
    x2 = _merge(oa.reshape(n, -1), ob.reshape(n, -1), oc.reshape(n, -1), p, x2,
                bf(W["proj_a"][l]), bf(W["proj_b"][l]), bf(W["proj_c"][l]), bf(W["out_proj"][l]),
                tm=tm)
    x3n, conv_new = _ffn(x2.reshape(nb, nt, d), conv_prev, W["ffn_norm"][l].reshape(1, d),
                         bf(W["ffn_up"][l]), W["ffn_conv_w"][l], W["ffn_conv_b"][l].reshape(1, -1),
                         bf(W["ffn_down"][l]), tm=_row_tile(nt, 256), fc=D_FF // 2)

    k_new = p3[:, :, COL_AK:COL_AK + 512].reshape(nb, nt, A_HEADS, 2, A_HEAD_DIM)
    v_new = p3[:, :, COL_AV:COL_AV + 512].reshape(nb, nt, A_HEADS, A_VDIM)
    shift_new = _rmsnorm(x3[:, -1], mix_g, tm=nb)
    return x3n, v_first, (k_new, v_new, shift_new, s_rwkv, s_hgrn, conv_new)


def kernel(x_prompt, x_sample, cache_attn_k, cache_attn_v, state_rwkv_shift, state_rwkv_wkv, state_hgrn, state_ffn_conv, mix_norm, in_proj, a_lq, a_lk, a_subln, b_mu, b_w0, b_w2, b_a0, b_a2, b_g2, b_k_k, b_k_a, b_r_k, b_ln_w, b_ln_b, b_v0, b_v1, b_v2, c_lb, c_norm, proj_a, proj_b, proj_c, out_proj, ffn_norm, ffn_up, ffn_conv_w, ffn_conv_b, ffn_down, final_norm):
    W = dict(mix_norm=mix_norm, in_proj=in_proj, a_lq=a_lq, a_lk=a_lk, a_subln=a_subln,
             b_mu=b_mu, b_w0=b_w0, b_w2=b_w2, b_a0=b_a0, b_a2=b_a2, b_g2=b_g2, b_k_k=b_k_k,
             b_k_a=b_k_a, b_r_k=b_r_k, b_ln_w=b_ln_w, b_ln_b=b_ln_b, b_v0=b_v0, b_v1=b_v1,
             b_v2=b_v2, c_lb=c_lb, c_norm=c_norm, proj_a=proj_a, proj_b=proj_b, proj_c=proj_c,
             out_proj=out_proj, ffn_norm=ffn_norm, ffn_up=ffn_up, ffn_conv_w=ffn_conv_w,
             ffn_conv_b=ffn_conv_b, ffn_down=ffn_down)
    depth = in_proj.shape[0]
    n_p = x_prompt.shape[0]
    xp, xs = x_prompt, x_sample
    vf_p = vf_s = None
    st_p, st_s = [], []
    for l in range(depth):
        w_in = _regroup_in_proj(in_proj[l]).astype(MXU_DTYPE)
        xp, vf_p, s_p = _block(
            xp, l, W, w_in, vf_p, None, None,
            jnp.zeros((n_p, B_HEADS, B_HEAD_DIM, B_HEAD_DIM), F32),
            jnp.zeros((n_p, C_HEADS, C_DIM, C_DIM), F32),
            jnp.zeros((n_p, FFN_CONV - 1, D_FF), F32), prompt=True)
        st_p.append(s_p)
        xs, vf_s, s_s = _block(
            xs, l, W, w_in, vf_s, state_rwkv_shift[l], (cache_attn_k[l], cache_attn_v[l]),
            state_rwkv_wkv[l], state_hgrn[l], state_ffn_conv[l], prompt=False)
        st_s.append(s_s)

    fin = final_norm.reshape(1, -1)
    d = xp.shape[-1]
    y_prompt = _rmsnorm(xp.reshape(-1, d), fin, tm=_row_tile(xp.shape[0] * xp.shape[1], 512))
    y_sample = _rmsnorm(xs.reshape(-1, d), fin, tm=_row_tile(xs.shape[0] * xs.shape[1], 512))
    stack = lambda states: [jnp.stack([s[i] for s in states]) for i in range(len(states[0]))]
    return (y_prompt.reshape(xp.shape), y_sample.reshape(xs.shape), *stack(st_p), *stack(st_s))
```

```python
import functools
import math

import numpy as np
import jax
import jax.numpy as jnp
from jax import lax
from jax.experimental import pallas as pl
from jax.experimental.pallas import tpu as pltpu

F32 = jnp.float32
MXU_DTYPE = jnp.bfloat16

NORM_EPS = 1e-6
NEG_BIG = -1e30
CHUNK = 64

A_HEADS = 4
A_HEAD_DIM = 64
A_VDIM = 128
A_SUBLN_EPS = 1e-5

B_HEADS = 8
B_HEAD_DIM = 64
B_W = 512
B_DECAY_LORA = 64
B_AAA_LORA = 64
B_MV_LORA = 32
B_GATE_LORA = 160
B_GN_EPS = 64e-5

C_HEADS = 4
C_DIM = 128
C_W = 512

D_FF = 2816
FFN_CONV = 3

LANE = 128
VMEM_LIMIT = 56 * 1024 * 1024

COL_GA, COL_GB, COL_GC = 0, 1024, 2048
COL_RKV = 3072
COL_AQ, COL_AK, COL_AV = 4608, 5120, 5632
COL_CQ, COL_CF, COL_CI, COL_COG = 6144, 6656, 7168, 7680
COL_GD = 8192
COL_WA = 8448
WP = 8704
GD_W = 256
WA_W = 128
RKV_W = 3 * B_W


def _cparams(sem):
    return pltpu.CompilerParams(dimension_semantics=sem, vmem_limit_bytes=VMEM_LIMIT)


def _sigmoid(x):
    return 1.0 / (1.0 + jnp.exp(-x))


def _rms(x, w, eps):
    return x * lax.rsqrt(jnp.mean(x * x, axis=-1, keepdims=True) + eps) * w


_NN = (((1,), (0,)), ((), ()))
_NT = (((1,), (1,)), ((), ()))
_TN = (((0,), (0,)), ((), ()))


def _dot(a, b, dims=_NN):
    return lax.dot_general(a.astype(MXU_DTYPE), b.astype(MXU_DTYPE), dims,
                           preferred_element_type=F32)


def _split2(a):
    hi = a.astype(MXU_DTYPE)
    lo = (a - hi.astype(F32)).astype(MXU_DTYPE)
    return hi, lo


def _dot3(a, b, dims=_NN):
    ah, al = _split2(a)
    bh, bl = _split2(b)
    d = lambda x, y: lax.dot_general(x, y, dims, preferred_element_type=F32)
    return d(ah, bh) + (d(ah, bl) + d(al, bh))


_RWKV_DOT = _dot


def _norm_matmul_kernel(x_ref, g_ref, w_ref, o_ref, *, apply_norm, tn):
    x = x_ref[...]
    if apply_norm:
        x = _rms(x, g_ref[...], NORM_EPS)
    xn = x.astype(w_ref.dtype)
    for j in range(w_ref.shape[1] // tn):
        cs = slice(j * tn, (j + 1) * tn)
        o_ref[:, cs] = jnp.dot(xn, w_ref[:, cs], preferred_element_type=F32)


def _norm_matmul(x, g, w, *, apply_norm, tm, tn):
    n, d = x.shape
    wn = w.shape[1]
    return pl.pallas_call(
        functools.partial(_norm_matmul_kernel, apply_norm=apply_norm, tn=tn),
        name="norm_matmul",
        grid=(n // tm,),
        in_specs=[pl.BlockSpec((tm, d), lambda i: (i, 0)),
                  pl.BlockSpec((1, d), lambda i: (0, 0)),
                  pl.BlockSpec((d, wn), lambda i: (0, 0), pipeline_mode=pl.Buffered(1))],
        out_specs=pl.BlockSpec((tm, wn), lambda i: (i, 0)),
        out_shape=jax.ShapeDtypeStruct((n, wn), F32),
        compiler_params=_cparams(("parallel",)),
    )(x, g, w)


def _rmsnorm_kernel(x_ref, g_ref, o_ref):
    o_ref[...] = _rms(x_ref[...], g_ref[...], NORM_EPS)


def _rmsnorm(x, g, *, tm):
    n, d = x.shape
    return pl.pallas_call(
        _rmsnorm_kernel,
        name="rmsnorm",
        grid=(n // tm,),
        in_specs=[pl.BlockSpec((tm, d), lambda i: (i, 0)),
                  pl.BlockSpec((1, d), lambda i: (0, 0))],
        out_specs=pl.BlockSpec((tm, d), lambda i: (i, 0)),
        out_shape=jax.ShapeDtypeStruct((n, d), F32),
        compiler_params=_cparams(("parallel",)),
    )(x, g)


def _map_queries(q, scale=A_HEAD_DIM ** -0.5):
    q = q * scale
    first = lax.broadcasted_iota(jnp.int32, q.shape, 1) < A_HEAD_DIM
    return (jnp.where(first, q, 0.0).astype(MXU_DTYPE),
            jnp.where(first, 0.0, q).astype(MXU_DTYPE))


def _diff_lambda(lq_ref, lk_ref, lam_init):
    e = jnp.exp(jnp.sum(lq_ref[...] * lk_ref[...], axis=-1, keepdims=True))
    return e[0:1] - e[1:2] + lam_init


def _attn_finish(o1, o2, lam, subln, lam_init):
    o = o1 - lam * o2
    return _rms(o, subln, A_SUBLN_EPS) * (1.0 - lam_init)


def _attn_prompt_kernel(q_ref, k_ref, v_ref, lq_ref, lk_ref, sub_ref, o_ref, *, tq, tk, lam_init):
    qi = pl.program_id(2)
    qs = _map_queries(q_ref[0], A_HEAD_DIM ** -0.5 * math.log2(math.e))
    log2_chunk = int(round(math.log2(CHUNK)))
    q_chunk = (qi * tq + lax.broadcasted_iota(jnp.int32, (tq, tk), 0)) >> log2_chunk
    ones = jnp.ones((tk, A_VDIM), MXU_DTYPE)

    def block(kstart, carry, masked):
        k = k_ref[0, pl.ds(kstart, tk), :].astype(MXU_DTYPE)
        v1 = jnp.concatenate([v_ref[0, pl.ds(kstart, tk), :].astype(MXU_DTYPE), ones], axis=1)
        if masked:
            k_chunk = (kstart + lax.broadcasted_iota(jnp.int32, (tq, tk), 1)) >> log2_chunk
            visible = k_chunk <= q_chunk
        new = []
        for c in range(2):
            m_prev, acc_prev = carry[c]
            s = lax.dot_general(qs[c], k, _NT, preferred_element_type=F32)
            if masked:
                s = jnp.where(visible, s, NEG_BIG)
            m_new = jnp.maximum(m_prev, jnp.max(s, axis=-1, keepdims=True))
            alpha = jnp.exp2(m_prev - m_new)
            p = jnp.exp2(s - m_new)
            acc_new = alpha * acc_prev + jnp.dot(p.astype(MXU_DTYPE), v1,
                                                 preferred_element_type=F32)
            new.append((m_new, acc_new))
        return tuple(new)

    init = tuple((jnp.full((tq, 1), NEG_BIG, F32), jnp.zeros((tq, 2 * A_VDIM), F32))
                 for _ in range(2))
    n_full = qi >> int(round(math.log2(tk // tq)))
    carry = lax.fori_loop(
        0, n_full, lambda i, c: block(pl.multiple_of(i * tk, tk), c, False), init)
    (_, acc1), (_, acc2) = block(pl.multiple_of(n_full * tk, tk), carry, True)
    lam = _diff_lambda(lq_ref, lk_ref, lam_init)
    o_ref[0] = _attn_finish(acc1[:, :A_VDIM] / acc1[:, A_VDIM:], acc2[:, :A_VDIM] / acc2[:, A_VDIM:],
                            lam, sub_ref[...], lam_init)


def _attn_prompt(p3, lq, lk, subln, *, lam_init, tq, tk):
    nb, nt, _ = p3.shape
    cq, ck, cv = COL_AQ // LANE, COL_AK // LANE, COL_AV // LANE
    const = lambda shape: pl.BlockSpec(shape, lambda b, h, qi: (0, 0))
    return pl.pallas_call(
        functools.partial(_attn_prompt_kernel, tq=tq, tk=tk, lam_init=lam_init),
        name="attn_prompt",
        grid=(nb, A_HEADS, nt // tq),
        in_specs=[
            pl.BlockSpec((1, tq, LANE), lambda b, h, qi: (b, qi, cq + h)),
            pl.BlockSpec((1, nt, LANE), lambda b, h, qi: (b, 0, ck + h)),
            pl.BlockSpec((1, nt, LANE), lambda b, h, qi: (b, 0, cv + h)),
            const((2, A_HEAD_DIM)), const((2, A_HEAD_DIM)), const((1, A_VDIM)),
        ],
        out_specs=pl.BlockSpec((1, tq, LANE), lambda b, h, qi: (b, qi, h)),
        out_shape=jax.ShapeDtypeStruct((nb, nt, A_HEADS * A_VDIM), F32),
        compiler_params=_cparams(("parallel", "parallel", "arbitrary")),
    )(p3, p3, p3, lq, lk, subln)


def _attn_sample_kernel(q_ref, kn_ref, vn_ref, kc_ref, vc_ref, lq_ref, lk_ref, sub_ref, o_ref,
                        *, lam_init):
    qs = _map_queries(q_ref[0])
    kn = kn_ref[0].astype(MXU_DTYPE)
    vn = vn_ref[0].astype(MXU_DTYPE)
    kc = kc_ref[0].astype(MXU_DTYPE)
    vc = vc_ref[0].astype(MXU_DTYPE)
    outs = []
    for c in range(2):
        s_c = lax.dot_general(qs[c], kc, _NT, preferred_element_type=F32)
        s_n = lax.dot_general(qs[c], kn, _NT, preferred_element_type=F32)
        m = jnp.maximum(jnp.max(s_c, axis=-1, keepdims=True), jnp.max(s_n, axis=-1, keepdims=True))
        p_c = jnp.exp(s_c - m)
        p_n = jnp.exp(s_n - m)
        l = jnp.sum(p_c, axis=-1, keepdims=True) + jnp.sum(p_n, axis=-1, keepdims=True)
        acc = (jnp.dot(p_c.astype(MXU_DTYPE), vc, preferred_element_type=F32)
               + jnp.dot(p_n.astype(MXU_DTYPE), vn, preferred_element_type=F32))
        outs.append(acc / l)
    lam = _diff_lambda(lq_ref, lk_ref, lam_init)
    o_ref[0] = _attn_finish(outs[0], outs[1], lam, sub_ref[...], lam_init)


def _attn_sample(p3, cache_k, cache_v, lq, lk, subln, *, lam_init):
    nb, nt, _ = p3.shape
    past = cache_k.shape[1]
    cq, ck, cv = COL_AQ // LANE, COL_AK // LANE, COL_AV // LANE
    return pl.pallas_call(
        functools.partial(_attn_sample_kernel, lam_init=lam_init),
        name="attn_sample",
        grid=(nb, A_HEADS),
        in_specs=[
            pl.BlockSpec((1, nt, LANE), lambda b, h: (b, 0, cq + h)),
            pl.BlockSpec((1, nt, LANE), lambda b, h: (b, 0, ck + h)),
            pl.BlockSpec((1, nt, LANE), lambda b, h: (b, 0, cv + h)),
            pl.BlockSpec((1, past, LANE), lambda b, h: (b, 0, h)),
            pl.BlockSpec((1, past, LANE), lambda b, h: (b, 0, h)),
            pl.BlockSpec((2, A_HEAD_DIM), lambda b, h: (0, 0)),
            pl.BlockSpec((2, A_HEAD_DIM), lambda b, h: (0, 0)),
            pl.BlockSpec((1, A_VDIM), lambda b, h: (0, 0)),
        ],
        out_specs=pl.BlockSpec((1, nt, LANE), lambda b, h: (b, 0, h)),
        out_shape=jax.ShapeDtypeStruct((nb, nt, A_HEADS * A_VDIM), F32),
        compiler_params=_cparams(("parallel", "parallel")),
    )(p3, p3, p3, cache_k, cache_v, lq, lk, subln)


def _cumsum_rows(x):
    n = x.shape[0]
    row = lax.broadcasted_iota(jnp.int32, x.shape, 0)
    s = 1
    while s < n:
        x = x + jnp.where(row >= s, pltpu.roll(x, s, 0), 0.0)
        s *= 2
    return x


def _shift_lerp(x, carry, mu):
    row = lax.broadcasted_iota(jnp.int32, x.shape, 0)
    prev = jnp.where(row == 0, carry, pltpu.roll(x, 1, 0))
    return x + (prev - x) * mu


def _rwkv_kernel(*refs, L, nc, vmix):
    it = iter(refs)
    rkv_ref, gd_ref, wa_ref = next(it), next(it), next(it)
    p_rkv_ref, p_gd_ref, p_wa_ref = next(it), next(it), next(it)
    s0_ref = next(it)
    vf_ref = next(it) if vmix else None
    mu_rkv, mu_gd, mu_wa = next(it), next(it), next(it)
    w0, w2p, a0, a2p, g2p = next(it), next(it), next(it), next(it), next(it)
    kk_w, ka_w, rk_w, lnw, lnb = next(it), next(it), next(it), next(it), next(it)
    if vmix:
        v0, v1p, v2p = next(it), next(it), next(it)
    o_ref = next(it)
    vout_ref = None if vmix else next(it)
    sout_ref = next(it)
    s_sc, c_rkv, c_gd, c_wa = next(it), next(it), next(it), next(it)

    t = pl.program_id(1)

    @pl.when(t == 0)
    def _():
        s_sc[...] = s0_ref[0]
        c_rkv[...] = p_rkv_ref[0]
        c_gd[...] = p_gd_ref[0]
        c_wa[...] = p_wa_ref[0]

    x_rkv, x_gd, x_wa = rkv_ref[0], gd_ref[0], wa_ref[0]
    xs_rkv = _shift_lerp(x_rkv, c_rkv[...], mu_rkv[...])
    xs_gd = _shift_lerp(x_gd, c_gd[...], mu_gd[...])
    xs_wa = _shift_lerp(x_wa, c_wa[...], mu_wa[...])
    tl = nc * L
    c_rkv[...] = x_rkv[tl - 1:tl]
    c_gd[...] = x_gd[tl - 1:tl]
    c_wa[...] = x_wa[tl - 1:tl]

    r = xs_rkv[:, 0:B_W]
    k = xs_rkv[:, B_W:2 * B_W]
    v = xs_rkv[:, 2 * B_W:3 * B_W]

    zw = -(w0[...] + _dot(jnp.tanh(xs_wa), w2p[...]))
    softplus = jnp.maximum(zw, 0.0) + jnp.log(1.0 + jnp.exp(-jnp.abs(zw)))
    lw = -jnp.exp(-softplus - 0.5)
    a = _sigmoid(a0[...] + _dot(xs_wa, a2p[...]))
    g = _dot(_sigmoid(xs_gd), g2p[...])
    if vmix:
        mix = _sigmoid(v0[...] + _dot(_dot(v, v1p[...]), v2p[...]))
        v = v + (vf_ref[0] - v) * mix
    else:
        vout_ref[0] = v

    kk_raw = k * kk_w[...]
    kp = k * (1.0 + (a - 1.0) * ka_w[...])

    chunks = range(nc)
    rs = [slice(c * L, (c + 1) * L) for c in chunks]
    G = [_cumsum_rows(lw[s]) for s in rs]
    e_g = [jnp.exp(G[c]) for c in chunks]
    e_gx = [jnp.exp(G[c] - lw[rs[c]]) for c in chunks]
    e_ng = [jnp.exp(-G[c]) for c in chunks]
    e_glg = [jnp.exp(G[c][L - 1:L] - G[c]) for c in chunks]
    e_gl = [jnp.exp(G[c][L - 1:L]) for c in chunks]

    row = lax.broadcasted_iota(jnp.int32, (L, L), 0)
    col = lax.broadcasted_iota(jnp.int32, (L, L), 1)
    strict = row > col
    eye = jnp.where(row == col, 1.0, 0.0)
    row2 = lax.broadcasted_iota(jnp.int32, (L, 2 * L), 0)
    col2 = lax.broadcasted_iota(jnp.int32, (L, 2 * L), 1)
    strict_k = (col2 >= L) & (row2 > col2 - L)
    incl2 = row2 >= (col2 & (L - 1))
    n_sq = int(round(math.log2(L))) - 1
    heads = range(B_HEADS)
    hs = [slice(h * B_HEAD_DIM, (h + 1) * B_HEAD_DIM) for h in heads]

    units = [(c, h) for c in chunks for h in heads]
    cut = lambda arr, c, h: arr[rs[c], hs[h]]
    r_u = {u_: cut(r, *u_) for u_ in units}
    v_u = {u_: cut(v, *u_) for u_ in units}
    kp_u = {u_: cut(kp, *u_) for u_ in units}
    kk_u, beta = {}, {}
    for u_ in units:
        t_kk = cut(kk_raw, *u_)
        kk_u[u_] = t_kk / jnp.maximum(
            jnp.sqrt(jnp.sum(t_kk * t_kk, axis=-1, keepdims=True)), 1e-12)
        beta[u_] = -(kk_u[u_] * cut(a, *u_))
    lhs = {(c, h): jnp.concatenate([kk_u[c, h] * e_gx[c][:, hs[h]],
                                    r_u[c, h] * e_g[c][:, hs[h]]], axis=0)
           for c, h in units}
    rhs = {(c, h): jnp.concatenate([beta[c, h] * e_ng[c][:, hs[h]],
                                    kp_u[c, h] * e_ng[c][:, hs[h]]], axis=0)
           for c, h in units}
    m = {u_: _RWKV_DOT(lhs[u_], rhs[u_], _NT) for u_ in units}
    n_pow = {u_: jnp.where(strict, m[u_][0:L, 0:L], 0.0) for u_ in units}
    top_k = {u_: jnp.where(strict_k, m[u_][0:L], 0.0) for u_ in units}
    bot = {u_: jnp.where(incl2, m[u_][L:2 * L], 0.0) for u_ in units}

    tinv = {u_: eye + n_pow[u_] for u_ in units}
    for _ in range(n_sq):
        n_pow = {u_: _RWKV_DOT(n_pow[u_], n_pow[u_]) for u_ in units}
        tinv = {u_: tinv[u_] + _RWKV_DOT(tinv[u_], n_pow[u_]) for u_ in units}
    vv = {u_: jnp.concatenate([v_u[u_], v_u[u_]], axis=0) for u_ in units}
    x_v = {u_: _RWKV_DOT(top_k[u_], vv[u_]) for u_ in units}
    bk = {(c, h): jnp.concatenate([beta[c, h] * e_glg[c][:, hs[h]],
                                   kp_u[c, h] * e_glg[c][:, hs[h]]], axis=0)
          for c, h in units}

    state = [s_sc[h] for h in heads]
    y = {}
    for c in chunks:
        ls = [_RWKV_DOT(lhs[c, h], state[h], _NT) for h in heads]
        u = [_RWKV_DOT(tinv[c, h], ls[h][0:L] + x_v[c, h]) for h in heads]
        uv = [jnp.concatenate([u[h], v_u[c, h]], axis=0) for h in heads]
        for h in heads:
            y[c, h] = ls[h][L:2 * L] + _RWKV_DOT(bot[c, h], uv[h])
        state = [state[h] * e_gl[c][:, hs[h]] + _RWKV_DOT(uv[h], bk[c, h], _TN) for h in heads]
    for h in heads:
        s_sc[h] = state[h]

    for c, h in units:
        sl = hs[h]
        mean = jnp.mean(y[c, h], axis=-1, keepdims=True)
        var = jnp.mean(jnp.square(y[c, h] - mean), axis=-1, keepdims=True)
        yn = (y[c, h] - mean) * lax.rsqrt(var + B_GN_EPS) * lnw[:, sl] + lnb[:, sl]
        yn = yn + jnp.sum(r_u[c, h] * kp_u[c, h] * rk_w[:, sl], axis=-1, keepdims=True) * v_u[c, h]
        o_ref[0, rs[c], sl] = yn * cut(g, c, h)

    @pl.when(t == pl.num_programs(1) - 1)
    def _():
        sout_ref[0] = s_sc[...]


def _rwkv(p3, prev3, s0, v_first, wts, *, L, nc):
    nb, nt, _ = p3.shape
    vmix = v_first is not None
    tl = nc * L
    c_rkv, c_gd, c_wa = COL_RKV // RKV_W, COL_GD // GD_W, COL_WA // WA_W
    const = lambda shape: pl.BlockSpec(shape, lambda b, t: (0,) * len(shape))
    in_specs = [
        pl.BlockSpec((1, tl, RKV_W), lambda b, t: (b, t, c_rkv)),
        pl.BlockSpec((1, tl, GD_W), lambda b, t: (b, t, c_gd)),
        pl.BlockSpec((1, tl, WA_W), lambda b, t: (b, t, c_wa)),
        pl.BlockSpec((1, 1, RKV_W), lambda b, t: (b, 0, c_rkv)),
        pl.BlockSpec((1, 1, GD_W), lambda b, t: (b, 0, c_gd)),
        pl.BlockSpec((1, 1, WA_W), lambda b, t: (b, 0, c_wa)),
        pl.BlockSpec((1, B_HEADS, B_HEAD_DIM, B_HEAD_DIM), lambda b, t: (b, 0, 0, 0)),
    ]
    args = [p3, p3, p3, prev3, prev3, prev3, s0]
    if vmix:
        in_specs.append(pl.BlockSpec((1, tl, B_W), lambda b, t: (b, t, 0)))
        args.append(v_first)
    names = ["mu_rkv", "mu_gd", "mu_wa", "w0", "w2p", "a0", "a2p", "g2p",
             "kk_w", "ka_w", "rk_w", "lnw", "lnb"]
    if vmix:
        names += ["v0", "v1p", "v2p"]
    for nme in names:
        in_specs.append(const(wts[nme].shape))
        args.append(wts[nme])
    seq_spec = pl.BlockSpec((1, tl, B_W), lambda b, t: (b, t, 0))
    seq_shape = jax.ShapeDtypeStruct((nb, nt, B_W), F32)
    st_spec = pl.BlockSpec((1, B_HEADS, B_HEAD_DIM, B_HEAD_DIM), lambda b, t: (b, 0, 0, 0))
    st_shape = jax.ShapeDtypeStruct((nb, B_HEADS, B_HEAD_DIM, B_HEAD_DIM), F32)
    if vmix:
        out_specs, out_shape = [seq_spec, st_spec], [seq_shape, st_shape]
    else:
        out_specs, out_shape = [seq_spec, seq_spec, st_spec], [seq_shape, seq_shape, st_shape]
    outs = pl.pallas_call(
        functools.partial(_rwkv_kernel, L=L, nc=nc, vmix=vmix),
        name="rwkv7",
        grid=(nb, nt // tl),
        in_specs=in_specs, out_specs=out_specs, out_shape=out_shape,
        scratch_shapes=[pltpu.VMEM((B_HEADS, B_HEAD_DIM, B_HEAD_DIM), F32),
                        pltpu.VMEM((1, RKV_W), F32), pltpu.VMEM((1, GD_W), F32),
                        pltpu.VMEM((1, WA_W), F32)],
        compiler_params=_cparams(("parallel", "arbitrary")),
    )(*args)
    if vmix:
        return outs[0], v_first, outs[1]
    return outs[0], outs[1], outs[2]


def _hgrn_levels(L):
    return [L >> (j + 1) for j in range(int(round(math.log2(L))))]


def _hgrn_sum_masks(L):
    t = np.arange(L)[:, None]
    i = np.arange(L)[None, :]
    blocks = []
    for b in _hgrn_levels(L):
        mid = (t // (2 * b)) * (2 * b) + b - 1
        upper = (t % (2 * b)) >= b
        blocks.append(upper & (i > mid) & (i <= t))
        blocks.append((~upper) & (i > t) & (i <= mid))
    blocks.append(i <= t)
    blocks.append(i > t)
    return np.concatenate(blocks, axis=0).astype(np.float32)


def _hgrn_kernel(cq_ref, cf_ref, ci_ref, cog_ref, s0_ref, clb_ref, cnorm_ref, mm_ref,
                 o_ref, sout_ref, st_sc, *, L, layer):
    t = pl.program_id(1)

    @pl.when(t == 0)
    def _():
        for h in range(C_HEADS):
            st_sc[h] = s0_ref[0, h].T

    clb = clb_ref[...]
    e = jnp.exp(clb - jnp.max(clb, axis=0, keepdims=True))
    soft = e / jnp.sum(e, axis=0, keepdims=True)
    lb = jnp.zeros((1, C_W), F32)
    for i in range(1, layer + 1):
        lb = lb + soft[i:i + 1]

    z, q, v = cf_ref[0], cq_ref[0], ci_ref[0]
    f = lb + (1.0 - lb) * _sigmoid(z)
    log_f = jnp.log(f)
    k = (1.0 - lb) * _sigmoid(-z)

    hi, lo = _split2(log_f)
    mm = mm_ref[...]
    sums = (jnp.dot(mm, hi, preferred_element_type=F32)
            + jnp.dot(mm, lo, preferred_element_type=F32))

    levels = _hgrn_levels(L)
    nl = len(levels)
    g_incl = sums[2 * nl * L:(2 * nl + 1) * L]
    g_suf = sums[(2 * nl + 1) * L:(2 * nl + 2) * L]
    g_last = g_incl[L - 1:L]
    q_in = q * jnp.exp(g_incl)
    k_out = k * jnp.exp(g_suf)
    e_last = jnp.exp(g_last)

    rowv = lax.broadcasted_iota(jnp.int32, (L, 1), 0)
    row = lax.broadcasted_iota(jnp.int32, (L, L), 0)
    col = lax.broadcasted_iota(jnp.int32, (L, L), 1)
    qt, kt, lmask = [], [], []
    for j, b in enumerate(levels):
        upper = (rowv & b) != 0
        qt.append(jnp.where(upper, q * jnp.exp(sums[2 * j * L:(2 * j + 1) * L]), 0.0))
        kt.append(jnp.where(upper, 0.0, k * jnp.exp(sums[(2 * j + 1) * L:(2 * j + 2) * L])))
        sh = int(round(math.log2(2 * b)))
        lmask.append(((row >> sh) == (col >> sh)) & ((row & b) != 0) & ((col & b) == 0))
    diag = row == col

    gate_in = cog_ref[0]
    heads = range(C_HEADS)
    hs = [slice(h * C_DIM, (h + 1) * C_DIM) for h in heads]
    a_mat = [jnp.where(diag, jnp.sum(q[:, s] * k[:, s], axis=-1, keepdims=True), 0.0) for s in hs]
    for j in range(nl):
        prod = [_dot(qt[j][:, s], kt[j][:, s], _NT) for s in hs]
        a_mat = [a_mat[h] + jnp.where(lmask[j], prod[h], 0.0) for h in heads]
    st = [st_sc[h] for h in heads]
    o = [_dot(a_mat[h], v[:, hs[h]]) + _dot(q_in[:, hs[h]], st[h], _NT) for h in heads]
    for h in heads:
        st_sc[h] = st[h] * e_last[:, hs[h]] + _dot(v[:, hs[h]], k_out[:, hs[h]], _TN)
    for h in heads:
        gate = gate_in[:, hs[h]]
        o_ref[0, :, hs[h]] = _rms(o[h], cnorm_ref[...], NORM_EPS) * (gate * _sigmoid(gate))

    @pl.when(t == pl.num_programs(1) - 1)
    def _():
        for h in range(C_HEADS):
            sout_ref[0, h] = st_sc[h].T


def _hgrn(p3, s0, c_lb, c_norm, *, L, layer):
    nb, nt, _ = p3.shape
    mm = jnp.asarray(_hgrn_sum_masks(L), MXU_DTYPE)
    cols = [COL_CQ // C_W, COL_CF // C_W, COL_CI // C_W, COL_COG // C_W]
    seq = lambda c: pl.BlockSpec((1, L, C_W), lambda b, t: (b, t, c))
    st_spec = pl.BlockSpec((1, C_HEADS, C_DIM, C_DIM), lambda b, t: (b, 0, 0, 0))
    return pl.pallas_call(
        functools.partial(_hgrn_kernel, L=L, layer=layer),
        name="hgrn2",
        grid=(nb, nt // L),
        in_specs=[seq(cols[0]), seq(cols[1]), seq(cols[2]), seq(cols[3]), st_spec,
                  pl.BlockSpec(c_lb.shape, lambda b, t: (0, 0)),
                  pl.BlockSpec((1, C_DIM), lambda b, t: (0, 0)),
                  pl.BlockSpec(mm.shape, lambda b, t: (0, 0))],
        out_specs=[pl.BlockSpec((1, L, C_W), lambda b, t: (b, t, 0)), st_spec],
        out_shape=[jax.ShapeDtypeStruct((nb, nt, C_W), F32),
                   jax.ShapeDtypeStruct((nb, C_HEADS, C_DIM, C_DIM), F32)],
        scratch_shapes=[pltpu.VMEM((C_HEADS, C_DIM, C_DIM), F32)],
        compiler_params=_cparams(("parallel", "arbitrary")),
    )(p3, p3, p3, p3, s0, c_lb, c_norm, mm)


def _merge_kernel(oa_ref, ob_ref, oc_ref, ga_ref, gb_ref, gc_ref, x_ref,
                  pa_ref, pb_ref, pc_ref, wo_ref, o_ref):
    m = (_sigmoid(ga_ref[...]) * _dot(oa_ref[...], pa_ref[...])
         + _sigmoid(gb_ref[...]) * _dot(ob_ref[...], pb_ref[...])
         + _sigmoid(gc_ref[...]) * _dot(oc_ref[...], pc_ref[...]))
    o_ref[...] = x_ref[...] + _dot(m, wo_ref[...])


def _merge(oa, ob, oc, p, x, pa, pb, pc, wo, *, tm):
    n, d = x.shape
    w = oa.shape[1]
    rows = lambda width, c: pl.BlockSpec((tm, width), lambda i: (i, c))
    const = lambda a: pl.BlockSpec(a.shape, lambda i: (0, 0))
    return pl.pallas_call(
        _merge_kernel,
        name="merge_out_proj",
        grid=(n // tm,),
        in_specs=[rows(w, 0), rows(w, 0), rows(w, 0),
                  rows(d, COL_GA // d), rows(d, COL_GB // d), rows(d, COL_GC // d),
                  rows(d, 0), const(pa), const(pb), const(pc), const(wo)],
        out_specs=rows(d, 0),
        out_shape=jax.ShapeDtypeStruct((n, d), F32),
        compiler_params=_cparams(("parallel",)),
    )(oa, ob, oc, p, p, p, x, pa, pb, pc, wo)


def _ffn_kernel(x_ref, prev_ref, g_ref, wup_ref, cw_ref, cb_ref, wdn_ref, o_ref, conv_ref,
                carry_sc, *, tm, fc):
    t = pl.program_id(1)

    @pl.when(t == 0)
    def _():
        carry_sc[...] = prev_ref[0]

    x = x_ref[0]
    xn = _rms(x, g_ref[...], NORM_EPS).astype(MXU_DTYPE)
    row = lax.broadcasted_iota(jnp.int32, (tm, fc), 0)
    acc = x
    for c in range(D_FF // fc):
        cs = slice(c * fc, (c + 1) * fc)
        up = jnp.dot(xn, wup_ref[:, cs], preferred_element_type=F32)
        gate = jnp.dot(xn, wup_ref[:, D_FF + c * fc:D_FF + (c + 1) * fc],
                       preferred_element_type=F32)
        old = carry_sc[:, cs]
        r1 = pltpu.roll(up, 1, 0)
        r2 = pltpu.roll(up, 2, 0)
        up_m1 = jnp.where(row == 0, old[1:2], r1)
        up_m2 = jnp.where(row == 0, old[0:1], jnp.where(row == 1, old[1:2], r2))
        carry_sc[:, cs] = r2[0:2]
        conv = (cb_ref[:, cs] + cw_ref[0:1, cs] * up_m2 + cw_ref[1:2, cs] * up_m1
                + cw_ref[2:3, cs] * up)
        hid = 0.5 * conv * (1.0 + lax.erf(conv * (2.0 ** -0.5))) * gate
        acc = acc + jnp.dot(hid.astype(MXU_DTYPE), wdn_ref[cs, :], preferred_element_type=F32)
    o_ref[0] = acc

    @pl.when(t == pl.num_programs(1) - 1)
    def _():
        conv_ref[0] = carry_sc[...]


def _ffn(x3, prev, g, wup, cw, cb, wdn, *, tm, fc):
    nb, nt, d = x3.shape
    const = lambda a: pl.BlockSpec(a.shape, lambda b, t: (0, 0))
    st_spec = pl.BlockSpec((1, FFN_CONV - 1, D_FF), lambda b, t: (b, 0, 0))
    return pl.pallas_call(
        functools.partial(_ffn_kernel, tm=tm, fc=fc),
        name="conv_ffn",
        grid=(nb, nt // tm),
        in_specs=[pl.BlockSpec((1, tm, d), lambda b, t: (b, t, 0)), st_spec,
                  const(g), const(wup), const(cw), const(cb), const(wdn)],
        out_specs=[pl.BlockSpec((1, tm, d), lambda b, t: (b, t, 0)), st_spec],
        out_shape=[jax.ShapeDtypeStruct((nb, nt, d), F32),
                   jax.ShapeDtypeStruct((nb, FFN_CONV - 1, D_FF), F32)],
        scratch_shapes=[pltpu.VMEM((FFN_CONV - 1, D_FF), F32)],
        compiler_params=_cparams(("parallel", "arbitrary")),
    )(x3, prev, g, wup, cw, cb, wdn)


def _regroup_in_proj(w):
    d = w.shape[0]
    z = lambda n: jnp.zeros((d, n), w.dtype)
    return jnp.concatenate(
        [w[:, 5408:8480], w[:, 1536:3072], w[:, 0:1536], w[:, 3360:5408],
         w[:, 3200:3360], z(GD_W - B_GATE_LORA), w[:, 3072:3200], z(WP - COL_WA - WA_W)], axis=1)


def _pad_rows(w, rows, at):
    out = jnp.zeros((rows, w.shape[1]), w.dtype)
    return out.at[at:at + w.shape[0]].set(w)


def _row_tile(n, pref):
    t = min(n, pref)
    while n % t:
        t //= 2
    return t


def _layer_weights(l, W):
    row = lambda a: a.reshape(1, -1)
    mu = W["b_mu"][l]
    wts = {
        "mu_rkv": row(mu[0:RKV_W]),
        "mu_wa": row(mu[RKV_W:RKV_W + WA_W]),
        "mu_gd": row(jnp.pad(mu[RKV_W + WA_W:], (0, GD_W - B_GATE_LORA))),
        "w0": row(W["b_w0"][l]),
        "w2p": _pad_rows(W["b_w2"][l], WA_W, 0).astype(MXU_DTYPE),
        "a0": row(W["b_a0"][l]),
        "a2p": _pad_rows(W["b_a2"][l], WA_W, B_DECAY_LORA).astype(MXU_DTYPE),
        "g2p": _pad_rows(W["b_g2"][l], GD_W, 0).astype(MXU_DTYPE),
        "kk_w": row(W["b_k_k"][l]), "ka_w": row(W["b_k_a"][l]), "rk_w": row(W["b_r_k"][l]),
        "lnw": row(W["b_ln_w"][l]), "lnb": row(W["b_ln_b"][l]),
    }
    if l > 0:
        wts["v0"] = row(W["b_v0"][l - 1])
        wts["v1p"] = jnp.pad(W["b_v1"][l - 1], ((0, 0), (0, LANE - B_MV_LORA))).astype(MXU_DTYPE)
        wts["v2p"] = _pad_rows(W["b_v2"][l - 1], LANE, 0).astype(MXU_DTYPE)
    return wts


def _block(x3, l, W, w_in, v_first, shift_row, kv_cache, s_rwkv, s_hgrn, conv_prev, *, prompt):
    nb, nt, d = x3.shape
    n = nb * nt
    x2 = x3.reshape(n, d)
    tm = _row_tile(n, 512)
    mix_g = W["mix_norm"][l].reshape(1, d)
    p = _norm_matmul(x2, mix_g, w_in, apply_norm=True, tm=_row_tile(n, 256), tn=512)
    p3 = p.reshape(nb, nt, WP)

    lam_init = 0.8 - 0.6 * math.exp(-0.3 * l)
    lq, lk = W["a_lq"][l], W["a_lk"][l]
    subln = W["a_subln"][l].reshape(1, A_VDIM)
    if prompt:
        oa = _attn_prompt(p3, lq, lk, subln, lam_init=lam_init, tq=_row_tile(nt, 512),
                          tk=_row_tile(nt, 1024))
    else:
        ck, cv = kv_cache
        past = ck.shape[1]
        oa = _attn_sample(p3, ck.reshape(nb, past, -1), cv.reshape(nb, past, -1), lq, lk, subln,
                          lam_init=lam_init)

    if shift_row is None:
        prev3 = jnp.zeros((nb, 1, WP), F32)
    else:
        prev3 = _norm_matmul(shift_row, mix_g, w_in, apply_norm=False,
                             tm=nb, tn=512).reshape(nb, 1, WP)
    L = min(CHUNK, nt)
    nc = 2 if nt % (2 * L) == 0 else 1
    ob, v_first, s_rwkv = _rwkv(p3, prev3, s_rwkv, v_first, _layer_weights(l, W), L=L, nc=nc)
    oc, s_hgrn = _hgrn(p3, s_hgrn, W["c_lb"], W["c_norm"][l].reshape(1, C_DIM), L=L, layer=l)

    bf = lambda a: a.astype(MXU_DTYPE)
    x2 = _merge(oa.reshape(n, -1), ob.reshape(n, -1), oc.reshape(n, -1), p, x2,
                bf(W["proj_a"][l]), bf(W["proj_b"][l]), bf(W["proj_c"][l]), bf(W["out_proj"][l]),
                tm=tm)
    x3n, conv_new = _ffn(x2.reshape(nb, nt, d), conv_prev, W["ffn_norm"][l].reshape(1, d),
                         bf(W["ffn_up"][l]), W["ffn_conv_w"][l], W["ffn_conv_b"][l].reshape(1, -1),
                         bf(W["ffn_down"][l]), tm=_row_tile(nt, 256), fc=D_FF // 2)

    k_new = p3[:, :, COL_AK:COL_AK + 512]
    v_new = p3[:, :, COL_AV:COL_AV + 512]
    shift_new = _rmsnorm(x3[:, -1], mix_g, tm=nb)
    return x3n, v_first, (k_new, v_new, shift_new, s_rwkv, s_hgrn, conv_new)


def kernel(x_prompt, x_sample, cache_attn_k, cache_attn_v, state_rwkv_shift, state_rwkv_wkv, state_hgrn, state_ffn_conv, mix_norm, in_proj, a_lq, a_lk, a_subln, b_mu, b_w0, b_w2, b_a0, b_a2, b_g2, b_k_k, b_k_a, b_r_k, b_ln_w, b_ln_b, b_v0, b_v1, b_v2, c_lb, c_norm, proj_a, proj_b, proj_c, out_proj, ffn_norm, ffn_up, ffn_conv_w, ffn_conv_b, ffn_down, final_norm):
    W = dict(mix_norm=mix_norm, in_proj=in_proj, a_lq=a_lq, a_lk=a_lk, a_subln=a_subln,
             b_mu=b_mu, b_w0=b_w0, b_w2=b_w2, b_a0=b_a0, b_a2=b_a2, b_g2=b_g2, b_k_k=b_k_k,
             b_k_a=b_k_a, b_r_k=b_r_k, b_ln_w=b_ln_w, b_ln_b=b_ln_b, b_v0=b_v0, b_v1=b_v1,
             b_v2=b_v2, c_lb=c_lb, c_norm=c_norm, proj_a=proj_a, proj_b=proj_b, proj_c=proj_c,
             out_proj=out_proj, ffn_norm=ffn_norm, ffn_up=ffn_up, ffn_conv_w=ffn_conv_w,
             ffn_conv_b=ffn_conv_b, ffn_down=ffn_down)
    depth = in_proj.shape[0]
    n_p = x_prompt.shape[0]
    xp, xs = x_prompt, x_sample
    vf_p = vf_s = None
    st_p, st_s = [], []
    for l in range(depth):
        w_in = _regroup_in_proj(in_proj[l]).astype(MXU_DTYPE)
        xp, vf_p, s_p = _block(
            xp, l, W, w_in, vf_p, None, None,
            jnp.zeros((n_p, B_HEADS, B_HEAD_DIM, B_HEAD_DIM), F32),
            jnp.zeros((n_p, C_HEADS, C_DIM, C_DIM), F32),
            jnp.zeros((n_p, FFN_CONV - 1, D_FF), F32), prompt=True)
        st_p.append(s_p)
        xs, vf_s, s_s = _block(
            xs, l, W, w_in, vf_s, state_rwkv_shift[l], (cache_attn_k[l], cache_attn_v[l]),
            state_rwkv_wkv[l], state_hgrn[l], state_ffn_conv[l], prompt=False)
        st_s.append(s_s)

    fin = final_norm.reshape(1, -1)
    d = xp.shape[-1]
    y_prompt = _rmsnorm(xp.reshape(-1, d), fin, tm=_row_tile(xp.shape[0] * xp.shape[1], 512))
    y_sample = _rmsnorm(xs.reshape(-1, d), fin, tm=_row_tile(xs.shape[0] * xs.shape[1], 512))

    def stack(states):
        out = [jnp.stack([s[i] for s in states]) for i in range(len(states[0]))]
        k, v = out[0], out[1]
        out[0] = k.reshape(*k.shape[:3], A_HEADS, 2, A_HEAD_DIM)
        out[1] = v.reshape(*v.shape[:3], A_HEADS, A_VDIM)
        return out

    return (y_prompt.reshape(xp.shape), y_sample.reshape(xs.shape), *stack(st_p), *stack(st_s))
```

```python
import functools
import math

import numpy as np
import jax
import jax.numpy as jnp
from jax import lax
from jax.experimental import pallas as pl
from jax.experimental.pallas import tpu as pltpu

F32 = jnp.float32
MXU_DTYPE = jnp.bfloat16

NORM_EPS = 1e-6
NEG_BIG = -1e30
CHUNK = 64

A_HEADS = 4
A_HEAD_DIM = 64
A_VDIM = 128
A_SUBLN_EPS = 1e-5

B_HEADS = 8
B_HEAD_DIM = 64
B_W = 512
B_DECAY_LORA = 64
B_AAA_LORA = 64
B_MV_LORA = 32
B_GATE_LORA = 160
B_GN_EPS = 64e-5

C_HEADS = 4
C_DIM = 128
C_W = 512

D_FF = 2816
FFN_CONV = 3

LANE = 128
VMEM_LIMIT = 56 * 1024 * 1024

COL_GA, COL_GB, COL_GC = 0, 1024, 2048
COL_RKV = 3072
COL_AQ, COL_AK, COL_AV = 4608, 5120, 5632
COL_CQ, COL_CF, COL_CI, COL_COG = 6144, 6656, 7168, 7680
COL_GD = 8192
COL_WA = 8448
WP = 8704
GD_W = 256
WA_W = 128
RKV_W = 3 * B_W


def _cparams(sem):
    return pltpu.CompilerParams(dimension_semantics=sem, vmem_limit_bytes=VMEM_LIMIT)


def _sigmoid(x):
    return 1.0 / (1.0 + jnp.exp(-x))


def _rms(x, w, eps):
    return x * lax.rsqrt(jnp.mean(x * x, axis=-1, keepdims=True) + eps) * w


_NN = (((1,), (0,)), ((), ()))
_NT = (((1,), (1,)), ((), ()))
_TN = (((0,), (0,)), ((), ()))


def _dot(a, b, dims=_NN):
    return lax.dot_general(a.astype(MXU_DTYPE), b.astype(MXU_DTYPE), dims,
                           preferred_element_type=F32)


def _split2(a):
    hi = a.astype(MXU_DTYPE)
    lo = (a - hi.astype(F32)).astype(MXU_DTYPE)
    return hi, lo


def _dot3(a, b, dims=_NN):
    ah, al = _split2(a)
    bh, bl = _split2(b)
    d = lambda x, y: lax.dot_general(x, y, dims, preferred_element_type=F32)
    return d(ah, bh) + (d(ah, bl) + d(al, bh))


_RWKV_DOT = _dot


def _norm_matmul_kernel(x_ref, g_ref, w_ref, o_ref, *, apply_norm, tn):
    x = x_ref[...]
    if apply_norm:
        x = _rms(x, g_ref[...], NORM_EPS)
    xn = x.astype(w_ref.dtype)
    for j in range(w_ref.shape[1] // tn):
        cs = slice(j * tn, (j + 1) * tn)
        o_ref[:, cs] = jnp.dot(xn, w_ref[:, cs], preferred_element_type=F32)


def _norm_matmul(x, g, w, *, apply_norm, tm, tn):
    n, d = x.shape
    wn = w.shape[1]
    return pl.pallas_call(
        functools.partial(_norm_matmul_kernel, apply_norm=apply_norm, tn=tn),
        name="norm_matmul",
        grid=(n // tm,),
        in_specs=[pl.BlockSpec((tm, d), lambda i: (i, 0)),
                  pl.BlockSpec((1, d), lambda i: (0, 0)),
                  pl.BlockSpec((d, wn), lambda i: (0, 0), pipeline_mode=pl.Buffered(1))],
        out_specs=pl.BlockSpec((tm, wn), lambda i: (i, 0)),
        out_shape=jax.ShapeDtypeStruct((n, wn), F32),
        compiler_params=_cparams(("parallel",)),
    )(x, g, w)


def _rmsnorm_kernel(x_ref, g_ref, o_ref):
    o_ref[...] = _rms(x_ref[...], g_ref[...], NORM_EPS)


def _rmsnorm(x, g, *, tm):
    n, d = x.shape
    return pl.pallas_call(
        _rmsnorm_kernel,
        name="rmsnorm",
        grid=(n // tm,),
        in_specs=[pl.BlockSpec((tm, d), lambda i: (i, 0)),
                  pl.BlockSpec((1, d), lambda i: (0, 0))],
        out_specs=pl.BlockSpec((tm, d), lambda i: (i, 0)),
        out_shape=jax.ShapeDtypeStruct((n, d), F32),
        compiler_params=_cparams(("parallel",)),
    )(x, g)


def _map_queries(q, scale=A_HEAD_DIM ** -0.5):
    q = q * scale
    first = lax.broadcasted_iota(jnp.int32, q.shape, 1) < A_HEAD_DIM
    return (jnp.where(first, q, 0.0).astype(MXU_DTYPE),
            jnp.where(first, 0.0, q).astype(MXU_DTYPE))


def _diff_lambda(lq_ref, lk_ref, lam_init):
    e = jnp.exp(jnp.sum(lq_ref[...] * lk_ref[...], axis=-1, keepdims=True))
    return e[0:1] - e[1:2] + lam_init


def _attn_finish(o1, o2, lam, subln, lam_init):
    o = o1 - lam * o2
    return _rms(o, subln, A_SUBLN_EPS) * (1.0 - lam_init)


def _attn_prompt_kernel(q_ref, k_ref, v_ref, lq_ref, lk_ref, sub_ref, o_ref, *, tq, tk, lam_init):
    qi = pl.program_id(2)
    qs = _map_queries(q_ref[0], A_HEAD_DIM ** -0.5 * math.log2(math.e))
    log2_chunk = int(round(math.log2(CHUNK)))
    q_chunk = (qi * tq + lax.broadcasted_iota(jnp.int32, (tq, tk), 0)) >> log2_chunk
    ones = jnp.ones((tk, A_VDIM), MXU_DTYPE)

    def block(kstart, carry, masked):
        k = k_ref[0, pl.ds(kstart, tk), :].astype(MXU_DTYPE)
        v1 = jnp.concatenate([v_ref[0, pl.ds(kstart, tk), :].astype(MXU_DTYPE), ones], axis=1)
        if masked:
            k_chunk = (kstart + lax.broadcasted_iota(jnp.int32, (tq, tk), 1)) >> log2_chunk
            visible = k_chunk <= q_chunk
        maps = range(2)
        s = [lax.dot_general(qs[c], k, _NT, preferred_element_type=F32) for c in maps]
        if masked:
            s = [jnp.where(visible, s[c], NEG_BIG) for c in maps]
        m_new = [jnp.maximum(carry[c][0], jnp.max(s[c], axis=-1, keepdims=True)) for c in maps]
        p = [jnp.exp2(s[c] - m_new[c]).astype(MXU_DTYPE) for c in maps]
        pv = [jnp.dot(p[c], v1, preferred_element_type=F32) for c in maps]
        return tuple((m_new[c], jnp.exp2(carry[c][0] - m_new[c]) * carry[c][1] + pv[c])
                     for c in maps)

    init = tuple((jnp.full((tq, 1), NEG_BIG, F32), jnp.zeros((tq, 2 * A_VDIM), F32))
                 for _ in range(2))
    n_full = qi >> int(round(math.log2(tk // tq)))
    carry = lax.fori_loop(
        0, n_full, lambda i, c: block(pl.multiple_of(i * tk, tk), c, False), init)
    (_, acc1), (_, acc2) = block(pl.multiple_of(n_full * tk, tk), carry, True)
    lam = _diff_lambda(lq_ref, lk_ref, lam_init)
    o_ref[0] = _attn_finish(acc1[:, :A_VDIM] / acc1[:, A_VDIM:], acc2[:, :A_VDIM] / acc2[:, A_VDIM:],
                            lam, sub_ref[...], lam_init)


def _attn_prompt(p3, lq, lk, subln, *, lam_init, tq, tk):
    nb, nt, _ = p3.shape
    cq, ck, cv = COL_AQ // LANE, COL_AK // LANE, COL_AV // LANE
    const = lambda shape: pl.BlockSpec(shape, lambda b, h, qi: (0, 0))
    return pl.pallas_call(
        functools.partial(_attn_prompt_kernel, tq=tq, tk=tk, lam_init=lam_init),
        name="attn_prompt",
        grid=(nb, A_HEADS, nt // tq),
        in_specs=[
            pl.BlockSpec((1, tq, LANE), lambda b, h, qi: (b, qi, cq + h)),
            pl.BlockSpec((1, nt, LANE), lambda b, h, qi: (b, 0, ck + h)),
            pl.BlockSpec((1, nt, LANE), lambda b, h, qi: (b, 0, cv + h)),
            const((2, A_HEAD_DIM)), const((2, A_HEAD_DIM)), const((1, A_VDIM)),
        ],
        out_specs=pl.BlockSpec((1, tq, LANE), lambda b, h, qi: (b, qi, h)),
        out_shape=jax.ShapeDtypeStruct((nb, nt, A_HEADS * A_VDIM), F32),
        compiler_params=_cparams(("parallel", "parallel", "arbitrary")),
    )(p3, p3, p3, lq, lk, subln)


def _attn_sample_kernel(q_ref, kn_ref, vn_ref, kc_ref, vc_ref, lq_ref, lk_ref, sub_ref, o_ref,
                        *, lam_init):
    qs = _map_queries(q_ref[0])
    kn = kn_ref[0].astype(MXU_DTYPE)
    vn = vn_ref[0].astype(MXU_DTYPE)
    kc = kc_ref[0].astype(MXU_DTYPE)
    vc = vc_ref[0].astype(MXU_DTYPE)
    outs = []
    for c in range(2):
        s_c = lax.dot_general(qs[c], kc, _NT, preferred_element_type=F32)
        s_n = lax.dot_general(qs[c], kn, _NT, preferred_element_type=F32)
        m = jnp.maximum(jnp.max(s_c, axis=-1, keepdims=True), jnp.max(s_n, axis=-1, keepdims=True))
        p_c = jnp.exp(s_c - m)
        p_n = jnp.exp(s_n - m)
        l = jnp.sum(p_c, axis=-1, keepdims=True) + jnp.sum(p_n, axis=-1, keepdims=True)
        acc = (jnp.dot(p_c.astype(MXU_DTYPE), vc, preferred_element_type=F32)
               + jnp.dot(p_n.astype(MXU_DTYPE), vn, preferred_element_type=F32))
        outs.append(acc / l)
    lam = _diff_lambda(lq_ref, lk_ref, lam_init)
    o_ref[0] = _attn_finish(outs[0], outs[1], lam, sub_ref[...], lam_init)


def _attn_sample(p3, cache_k, cache_v, lq, lk, subln, *, lam_init):
    nb, nt, _ = p3.shape
    past = cache_k.shape[1]
    cq, ck, cv = COL_AQ // LANE, COL_AK // LANE, COL_AV // LANE
    return pl.pallas_call(
        functools.partial(_attn_sample_kernel, lam_init=lam_init),
        name="attn_sample",
        grid=(nb, A_HEADS),
        in_specs=[
            pl.BlockSpec((1, nt, LANE), lambda b, h: (b, 0, cq + h)),
            pl.BlockSpec((1, nt, LANE), lambda b, h: (b, 0, ck + h)),
            pl.BlockSpec((1, nt, LANE), lambda b, h: (b, 0, cv + h)),
            pl.BlockSpec((1, past, LANE), lambda b, h: (b, 0, h)),
            pl.BlockSpec((1, past, LANE), lambda b, h: (b, 0, h)),
            pl.BlockSpec((2, A_HEAD_DIM), lambda b, h: (0, 0)),
            pl.BlockSpec((2, A_HEAD_DIM), lambda b, h: (0, 0)),
            pl.BlockSpec((1, A_VDIM), lambda b, h: (0, 0)),
        ],
        out_specs=pl.BlockSpec((1, nt, LANE), lambda b, h: (b, 0, h)),
        out_shape=jax.ShapeDtypeStruct((nb, nt, A_HEADS * A_VDIM), F32),
        compiler_params=_cparams(("parallel", "parallel")),
    )(p3, p3, p3, cache_k, cache_v, lq, lk, subln)


def _cumsum_rows(x):
    n = x.shape[0]
    row = lax.broadcasted_iota(jnp.int32, x.shape, 0)
    s = 1
    while s < n:
        x = x + jnp.where(row >= s, pltpu.roll(x, s, 0), 0.0)
        s *= 2
    return x


def _shift_lerp(x, carry, mu):
    row = lax.broadcasted_iota(jnp.int32, x.shape, 0)
    prev = jnp.where(row == 0, carry, pltpu.roll(x, 1, 0))
    return x + (prev - x) * mu


def _rwkv_kernel(*refs, L, nc, vmix):
    it = iter(refs)
    rkv_ref, gd_ref, wa_ref = next(it), next(it), next(it)
    p_rkv_ref, p_gd_ref, p_wa_ref = next(it), next(it), next(it)
    s0_ref = next(it)
    vf_ref = next(it) if vmix else None
    mu_rkv, mu_gd, mu_wa = next(it), next(it), next(it)
    w0, w2p, a0, a2p, g2p = next(it), next(it), next(it), next(it), next(it)
    kk_w, ka_w, rk_w, lnw, lnb = next(it), next(it), next(it), next(it), next(it)
    if vmix:
        v0, v1p, v2p = next(it), next(it), next(it)
    o_ref = next(it)
    vout_ref = None if vmix else next(it)
    sout_ref = next(it)
    s_sc, c_rkv, c_gd, c_wa = next(it), next(it), next(it), next(it)

    t = pl.program_id(1)

    @pl.when(t == 0)
    def _():
        s_sc[...] = s0_ref[0]
        c_rkv[...] = p_rkv_ref[0]
        c_gd[...] = p_gd_ref[0]
        c_wa[...] = p_wa_ref[0]

    x_rkv, x_gd, x_wa = rkv_ref[0], gd_ref[0], wa_ref[0]
    xs_rkv = _shift_lerp(x_rkv, c_rkv[...], mu_rkv[...])
    xs_gd = _shift_lerp(x_gd, c_gd[...], mu_gd[...])
    xs_wa = _shift_lerp(x_wa, c_wa[...], mu_wa[...])
    tl = nc * L
    c_rkv[...] = x_rkv[tl - 1:tl]
    c_gd[...] = x_gd[tl - 1:tl]
    c_wa[...] = x_wa[tl - 1:tl]

    r = xs_rkv[:, 0:B_W]
    k = xs_rkv[:, B_W:2 * B_W]
    v = xs_rkv[:, 2 * B_W:3 * B_W]

    zw = -(w0[...] + _dot(jnp.tanh(xs_wa), w2p[...]))
    softplus = jnp.maximum(zw, 0.0) + jnp.log(1.0 + jnp.exp(-jnp.abs(zw)))
    lw = -jnp.exp(-softplus - 0.5)
    a = _sigmoid(a0[...] + _dot(xs_wa, a2p[...]))
    g = _dot(_sigmoid(xs_gd), g2p[...])
    if vmix:
        mix = _sigmoid(v0[...] + _dot(_dot(v, v1p[...]), v2p[...]))
        v = v + (vf_ref[0] - v) * mix
    else:
        vout_ref[0] = v

    kk_raw = k * kk_w[...]
    kp = k * (1.0 + (a - 1.0) * ka_w[...])

    chunks = range(nc)
    rs = [slice(c * L, (c + 1) * L) for c in chunks]
    G = [_cumsum_rows(lw[s]) for s in rs]
    e_g = [jnp.exp(G[c]) for c in chunks]
    e_gx = [jnp.exp(G[c] - lw[rs[c]]) for c in chunks]
    e_ng = [jnp.exp(-G[c]) for c in chunks]
    e_glg = [jnp.exp(G[c][L - 1:L] - G[c]) for c in chunks]
    e_gl = [jnp.exp(G[c][L - 1:L]) for c in chunks]

    row = lax.broadcasted_iota(jnp.int32, (L, L), 0)
    col = lax.broadcasted_iota(jnp.int32, (L, L), 1)
    strict = row > col
    eye = jnp.where(row == col, 1.0, 0.0)
    row2 = lax.broadcasted_iota(jnp.int32, (L, 2 * L), 0)
    col2 = lax.broadcasted_iota(jnp.int32, (L, 2 * L), 1)
    strict_k = (col2 >= L) & (row2 > col2 - L)
    incl2 = row2 >= (col2 & (L - 1))
    n_sq = int(round(math.log2(L))) - 1
    heads = range(B_HEADS)
    hs = [slice(h * B_HEAD_DIM, (h + 1) * B_HEAD_DIM) for h in heads]

    units = [(c, h) for c in chunks for h in heads]
    cut = lambda arr, c, h: arr[rs[c], hs[h]]
    r_u = {u_: cut(r, *u_) for u_ in units}
    v_u = {u_: cut(v, *u_) for u_ in units}
    kp_u = {u_: cut(kp, *u_) for u_ in units}
    kk_u, beta = {}, {}
    for u_ in units:
        t_kk = cut(kk_raw, *u_)
        kk_u[u_] = t_kk / jnp.maximum(
            jnp.sqrt(jnp.sum(t_kk * t_kk, axis=-1, keepdims=True)), 1e-12)
        beta[u_] = -(kk_u[u_] * cut(a, *u_))
    lhs = {(c, h): jnp.concatenate([kk_u[c, h] * e_gx[c][:, hs[h]],
                                    r_u[c, h] * e_g[c][:, hs[h]]], axis=0)
           for c, h in units}
    rhs = {(c, h): jnp.concatenate([beta[c, h] * e_ng[c][:, hs[h]],
                                    kp_u[c, h] * e_ng[c][:, hs[h]]], axis=0)
           for c, h in units}
    m = {u_: _RWKV_DOT(lhs[u_], rhs[u_], _NT) for u_ in units}
    n_pow = {u_: jnp.where(strict, m[u_][0:L, 0:L], 0.0) for u_ in units}
    top_k = {u_: jnp.where(strict_k, m[u_][0:L], 0.0) for u_ in units}
    bot = {u_: jnp.where(incl2, m[u_][L:2 * L], 0.0) for u_ in units}

    tinv = {u_: eye + n_pow[u_] for u_ in units}
    for _ in range(n_sq):
        n_pow = {u_: _RWKV_DOT(n_pow[u_], n_pow[u_]) for u_ in units}
        tinv = {u_: tinv[u_] + _RWKV_DOT(tinv[u_], n_pow[u_]) for u_ in units}
    vv = {u_: jnp.concatenate([v_u[u_], v_u[u_]], axis=0) for u_ in units}
    x_v = {u_: _RWKV_DOT(top_k[u_], vv[u_]) for u_ in units}
    bk = {(c, h): jnp.concatenate([beta[c, h] * e_glg[c][:, hs[h]],
                                   kp_u[c, h] * e_glg[c][:, hs[h]]], axis=0)
          for c, h in units}

    state = [s_sc[h] for h in heads]
    y = {}
    for c in chunks:
        ls = [_RWKV_DOT(lhs[c, h], state[h], _NT) for h in heads]
        u = [_RWKV_DOT(tinv[c, h], ls[h][0:L] + x_v[c, h]) for h in heads]
        uv = [jnp.concatenate([u[h], v_u[c, h]], axis=0) for h in heads]
        for h in heads:
            y[c, h] = ls[h][L:2 * L] + _RWKV_DOT(bot[c, h], uv[h])
        state = [state[h] * e_gl[c][:, hs[h]] + _RWKV_DOT(uv[h], bk[c, h], _TN) for h in heads]
    for h in heads:
        s_sc[h] = state[h]

    for c, h in units:
        sl = hs[h]
        mean = jnp.mean(y[c, h], axis=-1, keepdims=True)
        var = jnp.mean(jnp.square(y[c, h] - mean), axis=-1, keepdims=True)
        yn = (y[c, h] - mean) * lax.rsqrt(var + B_GN_EPS) * lnw[:, sl] + lnb[:, sl]
        yn = yn + jnp.sum(r_u[c, h] * kp_u[c, h] * rk_w[:, sl], axis=-1, keepdims=True) * v_u[c, h]
        o_ref[0, rs[c], sl] = yn * cut(g, c, h)

    @pl.when(t == pl.num_programs(1) - 1)
    def _():
        sout_ref[0] = s_sc[...]


def _rwkv(p3, prev3, s0, v_first, wts, *, L, nc):
    nb, nt, _ = p3.shape
    vmix = v_first is not None
    tl = nc * L
    c_rkv, c_gd, c_wa = COL_RKV // RKV_W, COL_GD // GD_W, COL_WA // WA_W
    const = lambda shape: pl.BlockSpec(shape, lambda b, t: (0,) * len(shape))
    in_specs = [
        pl.BlockSpec((1, tl, RKV_W), lambda b, t: (b, t, c_rkv)),
        pl.BlockSpec((1, tl, GD_W), lambda b, t: (b, t, c_gd)),
        pl.BlockSpec((1, tl, WA_W), lambda b, t: (b, t, c_wa)),
        pl.BlockSpec((1, 1, RKV_W), lambda b, t: (b, 0, c_rkv)),
        pl.BlockSpec((1, 1, GD_W), lambda b, t: (b, 0, c_gd)),
        pl.BlockSpec((1, 1, WA_W), lambda b, t: (b, 0, c_wa)),
        pl.BlockSpec((1, B_HEADS, B_HEAD_DIM, B_HEAD_DIM), lambda b, t: (b, 0, 0, 0)),
    ]
    args = [p3, p3, p3, prev3, prev3, prev3, s0]
    if vmix:
        in_specs.append(pl.BlockSpec((1, tl, B_W), lambda b, t: (b, t, 0)))
        args.append(v_first)
    names = ["mu_rkv", "mu_gd", "mu_wa", "w0", "w2p", "a0", "a2p", "g2p",
             "kk_w", "ka_w", "rk_w", "lnw", "lnb"]
    if vmix:
        names += ["v0", "v1p", "v2p"]
    for nme in names:
        in_specs.append(const(wts[nme].shape))
        args.append(wts[nme])
    seq_spec = pl.BlockSpec((1, tl, B_W), lambda b, t: (b, t, 0))
    seq_shape = jax.ShapeDtypeStruct((nb, nt, B_W), F32)
    st_spec = pl.BlockSpec((1, B_HEADS, B_HEAD_DIM, B_HEAD_DIM), lambda b, t: (b, 0, 0, 0))
    st_shape = jax.ShapeDtypeStruct((nb, B_HEADS, B_HEAD_DIM, B_HEAD_DIM), F32)
    if vmix:
        out_specs, out_shape = [seq_spec, st_spec], [seq_shape, st_shape]
    else:
        out_specs, out_shape = [seq_spec, seq_spec, st_spec], [seq_shape, seq_shape, st_shape]
    outs = pl.pallas_call(
        functools.partial(_rwkv_kernel, L=L, nc=nc, vmix=vmix),
        name="rwkv7",
        grid=(nb, nt // tl),
        in_specs=in_specs, out_specs=out_specs, out_shape=out_shape,
        scratch_shapes=[pltpu.VMEM((B_HEADS, B_HEAD_DIM, B_HEAD_DIM), F32),
                        pltpu.VMEM((1, RKV_W), F32), pltpu.VMEM((1, GD_W), F32),
                        pltpu.VMEM((1, WA_W), F32)],
        compiler_params=_cparams(("parallel", "arbitrary")),
    )(*args)
    if vmix:
        return outs[0], v_first, outs[1]
    return outs[0], outs[1], outs[2]


def _hgrn_levels(L):
    return [L >> (j + 1) for j in range(int(round(math.log2(L))))]


def _hgrn_sum_masks(L):
    t = np.arange(L)[:, None]
    i = np.arange(L)[None, :]
    blocks = []
    for b in _hgrn_levels(L):
        mid = (t // (2 * b)) * (2 * b) + b - 1
        upper = (t % (2 * b)) >= b
        blocks.append(upper & (i > mid) & (i <= t))
        blocks.append((~upper) & (i > t) & (i <= mid))
    blocks.append(i <= t)
    blocks.append(i > t)
    return np.concatenate(blocks, axis=0).astype(np.float32)


def _hgrn_kernel(cq_ref, cf_ref, ci_ref, cog_ref, s0_ref, clb_ref, cnorm_ref, mm_ref,
                 o_ref, sout_ref, st_sc, *, L, layer):
    t = pl.program_id(1)

    @pl.when(t == 0)
    def _():
        for h in range(C_HEADS):
            st_sc[h] = s0_ref[0, h].T

    clb = clb_ref[...]
    e = jnp.exp(clb - jnp.max(clb, axis=0, keepdims=True))
    soft = e / jnp.sum(e, axis=0, keepdims=True)
    lb = jnp.zeros((1, C_W), F32)
    for i in range(1, layer + 1):
        lb = lb + soft[i:i + 1]

    z, q, v = cf_ref[0], cq_ref[0], ci_ref[0]
    f = lb + (1.0 - lb) * _sigmoid(z)
    log_f = jnp.log(f)
    k = (1.0 - lb) * _sigmoid(-z)

    hi, lo = _split2(log_f)
    mm = mm_ref[...]
    sums = (jnp.dot(mm, hi, preferred_element_type=F32)
            + jnp.dot(mm, lo, preferred_element_type=F32))

    levels = _hgrn_levels(L)
    nl = len(levels)
    g_incl = sums[2 * nl * L:(2 * nl + 1) * L]
    g_suf = sums[(2 * nl + 1) * L:(2 * nl + 2) * L]
    g_last = g_incl[L - 1:L]
    q_in = q * jnp.exp(g_incl)
    k_out = k * jnp.exp(g_suf)
    e_last = jnp.exp(g_last)

    rowv = lax.broadcasted_iota(jnp.int32, (L, 1), 0)
    row = lax.broadcasted_iota(jnp.int32, (L, L), 0)
    col = lax.broadcasted_iota(jnp.int32, (L, L), 1)
    qt, kt, lmask = [], [], []
    for j, b in enumerate(levels):
        upper = (rowv & b) != 0
        qt.append(jnp.where(upper, q * jnp.exp(sums[2 * j * L:(2 * j + 1) * L]), 0.0))
        kt.append(jnp.where(upper, 0.0, k * jnp.exp(sums[(2 * j + 1) * L:(2 * j + 2) * L])))
        sh = int(round(math.log2(2 * b)))
        lmask.append(((row >> sh) == (col >> sh)) & ((row & b) != 0) & ((col & b) == 0))
    diag = row == col

    gate_in = cog_ref[0]
    heads = range(C_HEADS)
    hs = [slice(h * C_DIM, (h + 1) * C_DIM) for h in heads]
    a_mat = [jnp.where(diag, jnp.sum(q[:, s] * k[:, s], axis=-1, keepdims=True), 0.0) for s in hs]
    for j in range(nl):
        prod = [_dot(qt[j][:, s], kt[j][:, s], _NT) for s in hs]
        a_mat = [a_mat[h] + jnp.where(lmask[j], prod[h], 0.0) for h in heads]
    st = [st_sc[h] for h in heads]
    o = [_dot(a_mat[h], v[:, hs[h]]) + _dot(q_in[:, hs[h]], st[h], _NT) for h in heads]
    for h in heads:
        st_sc[h] = st[h] * e_last[:, hs[h]] + _dot(v[:, hs[h]], k_out[:, hs[h]], _TN)
    for h in heads:
        gate = gate_in[:, hs[h]]
        o_ref[0, :, hs[h]] = _rms(o[h], cnorm_ref[...], NORM_EPS) * (gate * _sigmoid(gate))

    @pl.when(t == pl.num_programs(1) - 1)
    def _():
        for h in range(C_HEADS):
            sout_ref[0, h] = st_sc[h].T


def _hgrn(p3, s0, c_lb, c_norm, *, L, layer):
    nb, nt, _ = p3.shape
    mm = jnp.asarray(_hgrn_sum_masks(L), MXU_DTYPE)
    cols = [COL_CQ // C_W, COL_CF // C_W, COL_CI // C_W, COL_COG // C_W]
    seq = lambda c: pl.BlockSpec((1, L, C_W), lambda b, t: (b, t, c))
    st_spec = pl.BlockSpec((1, C_HEADS, C_DIM, C_DIM), lambda b, t: (b, 0, 0, 0))
    return pl.pallas_call(
        functools.partial(_hgrn_kernel, L=L, layer=layer),
        name="hgrn2",
        grid=(nb, nt // L),
        in_specs=[seq(cols[0]), seq(cols[1]), seq(cols[2]), seq(cols[3]), st_spec,
                  pl.BlockSpec(c_lb.shape, lambda b, t: (0, 0)),
                  pl.BlockSpec((1, C_DIM), lambda b, t: (0, 0)),
                  pl.BlockSpec(mm.shape, lambda b, t: (0, 0))],
        out_specs=[pl.BlockSpec((1, L, C_W), lambda b, t: (b, t, 0)), st_spec],
        out_shape=[jax.ShapeDtypeStruct((nb, nt, C_W), F32),
                   jax.ShapeDtypeStruct((nb, C_HEADS, C_DIM, C_DIM), F32)],
        scratch_shapes=[pltpu.VMEM((C_HEADS, C_DIM, C_DIM), F32)],
        compiler_params=_cparams(("parallel", "arbitrary")),
    )(p3, p3, p3, p3, s0, c_lb, c_norm, mm)


def _merge_kernel(oa_ref, ob_ref, oc_ref, ga_ref, gb_ref, gc_ref, x_ref,
                  pa_ref, pb_ref, pc_ref, wo_ref, o_ref):
    m = (_sigmoid(ga_ref[...]) * _dot(oa_ref[...], pa_ref[...])
         + _sigmoid(gb_ref[...]) * _dot(ob_ref[...], pb_ref[...])
         + _sigmoid(gc_ref[...]) * _dot(oc_ref[...], pc_ref[...]))
    o_ref[...] = x_ref[...] + _dot(m, wo_ref[...])


def _merge(oa, ob, oc, p, x, pa, pb, pc, wo, *, tm):
    n, d = x.shape
    w = oa.shape[1]
    rows = lambda width, c: pl.BlockSpec((tm, width), lambda i: (i, c))
    const = lambda a: pl.BlockSpec(a.shape, lambda i: (0, 0))
    return pl.pallas_call(
        _merge_kernel,
        name="merge_out_proj",
        grid=(n // tm,),
        in_specs=[rows(w, 0), rows(w, 0), rows(w, 0),
                  rows(d, COL_GA // d), rows(d, COL_GB // d), rows(d, COL_GC // d),
                  rows(d, 0), const(pa), const(pb), const(pc), const(wo)],
        out_specs=rows(d, 0),
        out_shape=jax.ShapeDtypeStruct((n, d), F32),
        compiler_params=_cparams(("parallel",)),
    )(oa, ob, oc, p, p, p, x, pa, pb, pc, wo)


def _ffn_kernel(x_ref, prev_ref, g_ref, wup_ref, cw_ref, cb_ref, wdn_ref, o_ref, conv_ref,
                carry_sc, *, tm, fc):
    t = pl.program_id(1)

    @pl.when(t == 0)
    def _():
        carry_sc[...] = prev_ref[0]

    x = x_ref[0]
    xn = _rms(x, g_ref[...], NORM_EPS).astype(MXU_DTYPE)
    row = lax.broadcasted_iota(jnp.int32, (tm, fc), 0)
    acc = x
    for c in range(D_FF // fc):
        cs = slice(c * fc, (c + 1) * fc)
        up = jnp.dot(xn, wup_ref[:, cs], preferred_element_type=F32)
        gate = jnp.dot(xn, wup_ref[:, D_FF + c * fc:D_FF + (c + 1) * fc],
                       preferred_element_type=F32)
        old = carry_sc[:, cs]
        r1 = pltpu.roll(up, 1, 0)
        r2 = pltpu.roll(up, 2, 0)
        up_m1 = jnp.where(row == 0, old[1:2], r1)
        up_m2 = jnp.where(row == 0, old[0:1], jnp.where(row == 1, old[1:2], r2))
        carry_sc[:, cs] = r2[0:2]
        conv = (cb_ref[:, cs] + cw_ref[0:1, cs] * up_m2 + cw_ref[1:2, cs] * up_m1
                + cw_ref[2:3, cs] * up)
        hid = 0.5 * conv * (1.0 + lax.erf(conv * (2.0 ** -0.5))) * gate
        acc = acc + jnp.dot(hid.astype(MXU_DTYPE), wdn_ref[cs, :], preferred_element_type=F32)
    o_ref[0] = acc

    @pl.when(t == pl.num_programs(1) - 1)
    def _():
        conv_ref[0] = carry_sc[...]


def _ffn(x3, prev, g, wup, cw, cb, wdn, *, tm, fc):
    nb, nt, d = x3.shape
    const = lambda a: pl.BlockSpec(a.shape, lambda b, t: (0, 0))
    st_spec = pl.BlockSpec((1, FFN_CONV - 1, D_FF), lambda b, t: (b, 0, 0))
    return pl.pallas_call(
        functools.partial(_ffn_kernel, tm=tm, fc=fc),
        name="conv_ffn",
        grid=(nb, nt // tm),
        in_specs=[pl.BlockSpec((1, tm, d), lambda b, t: (b, t, 0)), st_spec,
                  const(g), const(wup), const(cw), const(cb), const(wdn)],
        out_specs=[pl.BlockSpec((1, tm, d), lambda b, t: (b, t, 0)), st_spec],
        out_shape=[jax.ShapeDtypeStruct((nb, nt, d), F32),
                   jax.ShapeDtypeStruct((nb, FFN_CONV - 1, D_FF), F32)],
        scratch_shapes=[pltpu.VMEM((FFN_CONV - 1, D_FF), F32)],
        compiler_params=_cparams(("parallel", "arbitrary")),
    )(x3, prev, g, wup, cw, cb, wdn)


def _regroup_in_proj(w):
    d = w.shape[0]
    z = lambda n: jnp.zeros((d, n), w.dtype)
    return jnp.concatenate(
        [w[:, 5408:8480], w[:, 1536:3072], w[:, 0:1536], w[:, 3360:5408],
         w[:, 3200:3360], z(GD_W - B_GATE_LORA), w[:, 3072:3200], z(WP - COL_WA - WA_W)], axis=1)


def _pad_rows(w, rows, at):
    out = jnp.zeros((rows, w.shape[1]), w.dtype)
    return out.at[at:at + w.shape[0]].set(w)


def _row_tile(n, pref):
    t = min(n, pref)
    while n % t:
        t //= 2
    return t


def _layer_weights(l, W):
    row = lambda a: a.reshape(1, -1)
    mu = W["b_mu"][l]
    wts = {
        "mu_rkv": row(mu[0:RKV_W]),
        "mu_wa": row(mu[RKV_W:RKV_W + WA_W]),
        "mu_gd": row(jnp.pad(mu[RKV_W + WA_W:], (0, GD_W - B_GATE_LORA))),
        "w0": row(W["b_w0"][l]),
        "w2p": _pad_rows(W["b_w2"][l], WA_W, 0).astype(MXU_DTYPE),
        "a0": row(W["b_a0"][l]),
        "a2p": _pad_rows(W["b_a2"][l], WA_W, B_DECAY_LORA).astype(MXU_DTYPE),
        "g2p": _pad_rows(W["b_g2"][l], GD_W, 0).astype(MXU_DTYPE),
        "kk_w": row(W["b_k_k"][l]), "ka_w": row(W["b_k_a"][l]), "rk_w": row(W["b_r_k"][l]),
        "lnw": row(W["b_ln_w"][l]), "lnb": row(W["b_ln_b"][l]),
    }
    if l > 0:
        wts["v0"] = row(W["b_v0"][l - 1])
        wts["v1p"] = jnp.pad(W["b_v1"][l - 1], ((0, 0), (0, LANE - B_MV_LORA))).astype(MXU_DTYPE)
        wts["v2p"] = _pad_rows(W["b_v2"][l - 1], LANE, 0).astype(MXU_DTYPE)
    return wts


def _block(x3, l, W, w_in, v_first, shift_row, kv_cache, s_rwkv, s_hgrn, conv_prev, *, prompt):
    nb, nt, d = x3.shape
    n = nb * nt
    x2 = x3.reshape(n, d)
    tm = _row_tile(n, 512)
    mix_g = W["mix_norm"][l].reshape(1, d)
    p = _norm_matmul(x2, mix_g, w_in, apply_norm=True, tm=_row_tile(n, 256), tn=512)
    p3 = p.reshape(nb, nt, WP)

    lam_init = 0.8 - 0.6 * math.exp(-0.3 * l)
    lq, lk = W["a_lq"][l], W["a_lk"][l]
    subln = W["a_subln"][l].reshape(1, A_VDIM)
    if prompt:
        oa = _attn_prompt(p3, lq, lk, subln, lam_init=lam_init, tq=_row_tile(nt, 512),
                          tk=_row_tile(nt, 1024))
    else:
        ck, cv = kv_cache
        past = ck.shape[1]
        oa = _attn_sample(p3, ck.reshape(nb, past, -1), cv.reshape(nb, past, -1), lq, lk, subln,
                          lam_init=lam_init)

    if shift_row is None:
        prev3 = jnp.zeros((nb, 1, WP), F32)
    else:
        prev3 = _norm_matmul(shift_row, mix_g, w_in, apply_norm=False,
                             tm=nb, tn=512).reshape(nb, 1, WP)
    L = min(CHUNK, nt)
    nc = 2 if nt % (2 * L) == 0 else 1
    ob, v_first, s_rwkv = _rwkv(p3, prev3, s_rwkv, v_first, _layer_weights(l, W), L=L, nc=nc)
    oc, s_hgrn = _hgrn(p3, s_hgrn, W["c_lb"], W["c_norm"][l].reshape(1, C_DIM), L=L, layer=l)

    bf = lambda a: a.astype(MXU_DTYPE)
    x2 = _merge(oa.reshape(n, -1), ob.reshape(n, -1), oc.reshape(n, -1), p, x2,
                bf(W["proj_a"][l]), bf(W["proj_b"][l]), bf(W["proj_c"][l]), bf(W["out_proj"][l]),
                tm=tm)
    x3n, conv_new = _ffn(x2.reshape(nb, nt, d), conv_prev, W["ffn_norm"][l].reshape(1, d),
                         bf(W["ffn_up"][l]), W["ffn_conv_w"][l], W["ffn_conv_b"][l].reshape(1, -1),
                         bf(W["ffn_down"][l]), tm=_row_tile(nt, 256), fc=D_FF // 2)

    k_new = p3[:, :, COL_AK:COL_AK + 512]
    v_new = p3[:, :, COL_AV:COL_AV + 512]
    shift_new = _rmsnorm(x3[:, -1], mix_g, tm=nb)
    return x3n, v_first, (k_new, v_new, shift_new, s_rwkv, s_hgrn, conv_new)


def kernel(x_prompt, x_sample, cache_attn_k, cache_attn_v, state_rwkv_shift, state_rwkv_wkv, state_hgrn, state_ffn_conv, mix_norm, in_proj, a_lq, a_lk, a_subln, b_mu, b_w0, b_w2, b_a0, b_a2, b_g2, b_k_k, b_k_a, b_r_k, b_ln_w, b_ln_b, b_v0, b_v1, b_v2, c_lb, c_norm, proj_a, proj_b, proj_c, out_proj, ffn_norm, ffn_up, ffn_conv_w, ffn_conv_b, ffn_down, final_norm):
    W = dict(mix_norm=mix_norm, in_proj=in_proj, a_lq=a_lq, a_lk=a_lk, a_subln=a_subln,
             b_mu=b_mu, b_w0=b_w0, b_w2=b_w2, b_a0=b_a0, b_a2=b_a2, b_g2=b_g2, b_k_k=b_k_k,
             b_k_a=b_k_a, b_r_k=b_r_k, b_ln_w=b_ln_w, b_ln_b=b_ln_b, b_v0=b_v0, b_v1=b_v1,
             b_v2=b_v2, c_lb=c_lb, c_norm=c_norm, proj_a=proj_a, proj_b=proj_b, proj_c=proj_c,
             out_proj=out_proj, ffn_norm=ffn_norm, ffn_up=ffn_up, ffn_conv_w=ffn_conv_w,
             ffn_conv_b=ffn_conv_b, ffn_down=ffn_down)
    depth = in_proj.shape[0]
    n_p = x_prompt.shape[0]
    xp, xs = x_prompt, x_sample
    vf_p = vf_s = None
    st_p, st_s = [], []
    for l in range(depth):
        w_in = _regroup_in_proj(in_proj[l]).astype(MXU_DTYPE)
        xp, vf_p, s_p = _block(
            xp, l, W, w_in, vf_p, None, None,
            jnp.zeros((n_p, B_HEADS, B_HEAD_DIM, B_HEAD_DIM), F32),
            jnp.zeros((n_p, C_HEADS, C_DIM, C_DIM), F32),
            jnp.zeros((n_p, FFN_CONV - 1, D_FF), F32), prompt=True)
        st_p.append(s_p)
        xs, vf_s, s_s = _block(
            xs, l, W, w_in, vf_s, state_rwkv_shift[l], (cache_attn_k[l], cache_attn_v[l]),
            state_rwkv_wkv[l], state_hgrn[l], state_ffn_conv[l], prompt=False)
        st_s.append(s_s)

    fin = final_norm.reshape(1, -1)
    d = xp.shape[-1]
    y_prompt = _rmsnorm(xp.reshape(-1, d), fin, tm=_row_tile(xp.shape[0] * xp.shape[1], 512))
    y_sample = _rmsnorm(xs.reshape(-1, d), fin, tm=_row_tile(xs.shape[0] * xs.shape[1], 512))

    def stack(states):
        out = [jnp.stack([s[i] for s in states]) for i in range(len(states[0]))]
        k, v = out[0], out[1]
        out[0] = k.reshape(*k.shape[:3], A_HEADS, 2, A_HEAD_DIM)
        out[1] = v.reshape(*v.shape[:3], A_HEADS, A_VDIM)
        return out

    return (y_prompt.reshape(xp.shape), y_sample.reshape(xs.shape), *stack(st_p), *stack(st_s))
```
